```python
import math
import jax, jax.numpy as jnp
from jax import lax
import numpy as np


D_MODEL = 4096
BATCH = 2
SEQ = 8192
DEPTH = 4

N_META = 16
NORM_EPS = 1e-5

CONV_WIDTH = D_MODEL // 4
CONV_K = 3

ATTN_HEAD_DIM = 128
ATTN_WIDTH = 3 * D_MODEL // 8
ATTN_HEADS = ATTN_WIDTH // ATTN_HEAD_DIM
ATTN_KV_HEADS = 4
ATTN_GROUP = ATTN_HEADS // ATTN_KV_HEADS
ATTN_KV_WIDTH = ATTN_KV_HEADS * ATTN_HEAD_DIM
WINDOW = 128
BLOCK = 128

RWKV_HEAD_DIM = 64
RWKV_WIDTH = D_MODEL - CONV_WIDTH - ATTN_WIDTH
RWKV_HEADS = RWKV_WIDTH // RWKV_HEAD_DIM
DECAY_RANK = max(32, int(round(D_MODEL ** 0.5 * 1.8 / 32)) * 32)
ICLR_RANK = max(32, int(round(D_MODEL ** 0.5 * 1.8 / 32)) * 32)
GATE_RANK = max(32, int(round(D_MODEL ** 0.8 * 0.6 / 32)) * 32)
GN_EPS = 64e-5

MIX_WIDTH = CONV_WIDTH + ATTN_WIDTH + RWKV_WIDTH
CONV_COLS = 3 * CONV_WIDTH
ATTN_COLS = ATTN_WIDTH + 2 * ATTN_KV_WIDTH
RWKV_COLS = 3 * RWKV_WIDTH + DECAY_RANK + ICLR_RANK + GATE_RANK
IN_COLS = CONV_COLS + ATTN_COLS + RWKV_COLS

FFN_HIDDEN = -(-8 * D_MODEL // (3 * 256)) * 256

kernel_name = 'hybrid_conv_swa_rwkv7_block'


def alibi_slopes(n):
    def pow2_slopes(m):
        start = 2.0 ** (-8.0 / m)
        return [start ** (i + 1) for i in range(m)]
    c = 2 ** int(math.floor(math.log2(n)))
    s = pow2_slopes(c)
    if c < n:
        s = s + pow2_slopes(2 * c)[0::2][: n - c]
    return np.asarray(s, dtype=np.float32)


def split_cols(t, sizes):
    idx = np.cumsum(sizes)[:-1].tolist()
    return jnp.split(t, idx, axis=-1)


def rmsnorm(x, g):
    xf = x.astype(jnp.float32)
    y = xf * lax.rsqrt(jnp.mean(xf * xf, axis=-1, keepdims=True) + NORM_EPS)
    return (y * g.astype(jnp.float32)).astype(x.dtype)


def shift_right(t):
    return jnp.pad(t, ((0, 0), (1, 0), (0, 0)))[:, :-1]


def short_conv_mixer(b_gate, c_gate, h, conv_w):
    u = c_gate * h
    t = u.shape[1]
    up = jnp.pad(u, ((0, 0), (CONV_K - 1, 0), (0, 0)))
    y = up[:, 0:t] * conv_w[:, 0]
    for j in range(1, CONV_K):
        y = y + up[:, j:j + t] * conv_w[:, j]
    return b_gate * y


def sliding_window_attention(q, k, v, sinks):
    b, t, _ = q.shape
    D, KVH, G = ATTN_HEAD_DIM, ATTN_KV_HEADS, ATTN_GROUP
    pad = BLOCK - N_META
    nb = (t + pad) // BLOCK
    f32 = jnp.float32

    def to_blocks(z, h):
        z = jnp.pad(z.astype(f32), ((0, 0), (pad, 0), (0, 0)))
        return z.reshape(b, nb, BLOCK, h, D)

    qb = to_blocks(q, ATTN_HEADS).reshape(b, nb, BLOCK, KVH, G, D) * (D ** -0.5)
    kb, vb = to_blocks(k, KVH), to_blocks(v, KVH)

    def with_prev(z):
        prev = jnp.pad(z[:, :-1], ((0, 0), (1, 0), (0, 0), (0, 0), (0, 0)))
        return jnp.concatenate([prev, z], axis=2)

    k_band, v_band = with_prev(kb), with_prev(vb)
    k_meta = k[:, :N_META].astype(f32).reshape(b, N_META, KVH, D)
    v_meta = v[:, :N_META].astype(f32).reshape(b, N_META, KVH, D)

    s_band = jnp.einsum('bnqhgd,bnkhd->bnhgqk', qb, k_band)
    s_meta = jnp.einsum('bnqhgd,bmhd->bnhgqm', qb, k_meta)

    start = jnp.arange(nb)[:, None] * BLOCK - pad
    q_pos = start + jnp.arange(BLOCK)[None, :]
    k_pos = start + jnp.arange(-BLOCK, BLOCK)[None, :]
    dist = q_pos[:, :, None] - k_pos[:, None, :]
    band_ok = (dist >= 0) & (dist < WINDOW) & (k_pos[:, None, :] >= N_META)
    slopes = jnp.asarray(alibi_slopes(ATTN_HEADS)).reshape(KVH, G)
    s_band = s_band - slopes[:, :, None, None] * dist[:, None, None].astype(f32)
    s_band = jnp.where(band_ok[:, None, None], s_band, -jnp.inf)
    meta_ok = jnp.arange(N_META)[None, None, :] <= q_pos[:, :, None]
    s_meta = jnp.where(meta_ok[:, None, None], s_meta, -jnp.inf)
    s_sink = jnp.broadcast_to(sinks.astype(f32).reshape(KVH, G)[:, :, None, None],
                              s_band.shape[:-1] + (1,))

    probs = jax.nn.softmax(jnp.concatenate([s_band, s_meta, s_sink], axis=-1), axis=-1)
    p_band = probs[..., :2 * BLOCK]
    p_meta = probs[..., 2 * BLOCK:2 * BLOCK + N_META]
    o = (jnp.einsum('bnhgqk,bnkhd->bnqhgd', p_band, v_band)
         + jnp.einsum('bnhgqm,bmhd->bnqhgd', p_meta, v_meta))
    return o.reshape(b, nb * BLOCK, ATTN_WIDTH)[:, pad:].astype(q.dtype)


def rwkv7_time_mix(p, mu, w0, w2, a0, a2, g2, k_k, k_a, r_k, ln_w, ln_b):
    b, t, _ = p.shape
    H, N = RWKV_HEADS, RWKV_HEAD_DIM
    f32 = jnp.float32
    p = p + (shift_right(p) - p) * mu
    r, k, v, wl, al, gl = split_cols(p, [RWKV_WIDTH] * 3 + [DECAY_RANK, ICLR_RANK, GATE_RANK])
    w = -jax.nn.softplus(-(w0 + jnp.tanh(wl) @ w2)) - 0.5
    a = jax.nn.sigmoid(a0 + al @ a2)
    g = jax.nn.sigmoid(gl) @ g2

    def heads(z):
        return z.astype(f32).reshape(b, t, H, N)

    r, k, v, a, w = heads(r), heads(k), heads(v), heads(a), heads(w)
    decay = jnp.exp(-jnp.exp(w))
    kk = k * k_k.astype(f32).reshape(H, N)
    kk = kk / jnp.maximum(jnp.sqrt(jnp.sum(kk * kk, axis=-1, keepdims=True)), 1e-12)
    k = k * (1.0 + (a - 1.0) * k_a.astype(f32).reshape(H, N))

    def step(S, inp):
        d_t, r_t, k_t, v_t, kk_t, a_t = inp
        sa = jnp.einsum('bhvk,bhk->bhv', S, -kk_t)
        S = (S * d_t[:, :, None, :]
             + sa[..., None] * (kk_t * a_t)[:, :, None, :]
             + v_t[..., None] * k_t[:, :, None, :])
        return S, jnp.einsum('bhvk,bhk->bhv', S, r_t)

    def tm(z):
        return jnp.swapaxes(z, 0, 1)

    S0 = jnp.zeros((b, H, N, N), f32)
    _, y = lax.scan(step, S0, (tm(decay), tm(r), tm(k), tm(v), tm(kk), tm(a)))
    y = tm(y)
    mean = jnp.mean(y, axis=-1, keepdims=True)
    var = jnp.mean(jnp.square(y - mean), axis=-1, keepdims=True)
    y = (y - mean) * lax.rsqrt(var + GN_EPS)
    y = y * ln_w.astype(f32).reshape(H, N) + ln_b.astype(f32).reshape(H, N)
    y = y + jnp.sum(r * k * r_k.astype(f32).reshape(H, N), axis=-1, keepdims=True) * v
    return (y.reshape(b, t, RWKV_WIDTH) * g.astype(f32)).astype(p.dtype)


def setup_inputs(seed: int = 0) -> dict:
    key = jax.random.key(seed)
    ks = jax.random.split(key, 24)
    f32 = jnp.float32

    def nrm(k, shape, scale):
        return jax.random.normal(k, shape, f32) * scale

    x = nrm(ks[0], (BATCH, SEQ, D_MODEL), 1.0)
    meta_tokens = nrm(ks[1], (N_META, D_MODEL), 1.0)
    norm_mix = 1.0 + nrm(ks[2], (DEPTH, D_MODEL), 0.02)
    w_in = nrm(ks[3], (DEPTH, D_MODEL, IN_COLS), D_MODEL ** -0.5)
    conv_w = nrm(ks[4], (DEPTH, CONV_WIDTH, CONV_K), CONV_K ** -0.5)
    attn_sinks = nrm(ks[5], (DEPTH, ATTN_HEADS), 0.5)
    rwkv_mu = jax.random.uniform(ks[6], (DEPTH, RWKV_COLS), f32)
    rwkv_w0 = jax.random.uniform(ks[7], (DEPTH, RWKV_WIDTH), f32, -6.0, 1.0)
    rwkv_w2 = nrm(ks[8], (DEPTH, DECAY_RANK, RWKV_WIDTH), 0.5 * DECAY_RANK ** -0.5)
    rwkv_a0 = nrm(ks[9], (DEPTH, RWKV_WIDTH), 0.1)
    rwkv_a2 = nrm(ks[10], (DEPTH, ICLR_RANK, RWKV_WIDTH), ICLR_RANK ** -0.5)
    rwkv_g2 = nrm(ks[11], (DEPTH, GATE_RANK, RWKV_WIDTH), GATE_RANK ** -0.5)
    rwkv_k_k = 0.85 + nrm(ks[12], (DEPTH, RWKV_WIDTH), 0.05)
    rwkv_k_a = 1.0 + nrm(ks[13], (DEPTH, RWKV_WIDTH), 0.05)
    rwkv_r_k = nrm(ks[14], (DEPTH, RWKV_WIDTH), 0.1)
    rwkv_ln_w = 1.0 + nrm(ks[15], (DEPTH, RWKV_WIDTH), 0.02)
    rwkv_ln_b = nrm(ks[16], (DEPTH, RWKV_WIDTH), 0.02)
    w_out = nrm(ks[17], (DEPTH, MIX_WIDTH, D_MODEL), MIX_WIDTH ** -0.5)
    norm_ffn = 1.0 + nrm(ks[18], (DEPTH, D_MODEL), 0.02)
    w_ffn_in = nrm(ks[19], (DEPTH, D_MODEL, 2 * FFN_HIDDEN), D_MODEL ** -0.5)
    w_ffn_out = nrm(ks[20], (DEPTH, FFN_HIDDEN, D_MODEL), FFN_HIDDEN ** -0.5)
    norm_final = 1.0 + nrm(ks[21], (D_MODEL,), 0.02)
    return {'x': x, 'meta_tokens': meta_tokens, 'norm_mix': norm_mix, 'w_in': w_in,
            'conv_w': conv_w, 'attn_sinks': attn_sinks, 'rwkv_mu': rwkv_mu,
            'rwkv_w0': rwkv_w0, 'rwkv_w2': rwkv_w2, 'rwkv_a0': rwkv_a0, 'rwkv_a2': rwkv_a2,
            'rwkv_g2': rwkv_g2, 'rwkv_k_k': rwkv_k_k, 'rwkv_k_a': rwkv_k_a,
            'rwkv_r_k': rwkv_r_k, 'rwkv_ln_w': rwkv_ln_w, 'rwkv_ln_b': rwkv_ln_b,
            'w_out': w_out, 'norm_ffn': norm_ffn, 'w_ffn_in': w_ffn_in,
            'w_ffn_out': w_ffn_out, 'norm_final': norm_final}


def reference(x, meta_tokens, norm_mix, w_in, conv_w, attn_sinks, rwkv_mu, rwkv_w0,
              rwkv_w2, rwkv_a0, rwkv_a2, rwkv_g2, rwkv_k_k, rwkv_k_a, rwkv_r_k,
              rwkv_ln_w, rwkv_ln_b, w_out, norm_ffn, w_ffn_in, w_ffn_out, norm_final):
    b = x.shape[0]
    meta = jnp.broadcast_to(meta_tokens.astype(x.dtype)[None], (b, N_META, D_MODEL))
    h = jnp.concatenate([meta, x], axis=1)
    for l in range(DEPTH):
        u = rmsnorm(h, norm_mix[l])
        p = u @ w_in[l]
        p_conv, p_attn, p_rwkv = split_cols(p, [CONV_COLS, ATTN_COLS, RWKV_COLS])
        c_b, c_c, c_h = split_cols(p_conv, [CONV_WIDTH] * 3)
        q, k, v = split_cols(p_attn, [ATTN_WIDTH, ATTN_KV_WIDTH, ATTN_KV_WIDTH])
        y_conv = short_conv_mixer(c_b, c_c, c_h, conv_w[l])
        y_attn = sliding_window_attention(q, k, v, attn_sinks[l])
        y_rwkv = rwkv7_time_mix(p_rwkv, rwkv_mu[l], rwkv_w0[l], rwkv_w2[l], rwkv_a0[l],
                                rwkv_a2[l], rwkv_g2[l], rwkv_k_k[l], rwkv_k_a[l],
                                rwkv_r_k[l], rwkv_ln_w[l], rwkv_ln_b[l])
        h = h + jnp.concatenate([y_conv, y_attn, y_rwkv], axis=-1) @ w_out[l]
        u = rmsnorm(h, norm_ffn[l])
        gate, up = split_cols(u @ w_ffn_in[l], [FFN_HIDDEN, FFN_HIDDEN])
        h = h + (jax.nn.silu(gate) * up) @ w_ffn_out[l]
    return rmsnorm(h, norm_final)[:, N_META:]
```

```python
import functools
import math

import numpy as np
import jax
import jax.numpy as jnp
from jax import lax
from jax.experimental import pallas as pl
from jax.experimental.pallas import tpu as pltpu

F32 = jnp.float32
BF16 = jnp.bfloat16

N_META = 16
NORM_EPS = 1e-5
GN_EPS = 64e-5
CONV_K = 3
HEAD_DIM = 128
ATTN_HEADS = 12
ATTN_KV_HEADS = 4
ATTN_GROUP = ATTN_HEADS // ATTN_KV_HEADS
WINDOW = 128
RWKV_N = 64
DECAY_RANK = 128
ICLR_RANK = 128
GATE_RANK = 480

LANES = 128
ROW_BLOCK = 128
CHUNK = 64
VMEM_LIMIT = 56 * 1024 * 1024
NEG = -1e30


def _alibi_slopes(n):
    def pow2_slopes(m):
        start = 2.0 ** (-8.0 / m)
        return [start ** (i + 1) for i in range(m)]
    c = 2 ** int(math.floor(math.log2(n)))
    s = pow2_slopes(c)
    if c < n:
        s = s + pow2_slopes(2 * c)[0::2][: n - c]
    return [float(np.float32(v)) for v in s]


def _cparams(sem):
    return pltpu.CompilerParams(dimension_semantics=sem, vmem_limit_bytes=VMEM_LIMIT)


def _dot(a, b):
    return jnp.dot(a, b, preferred_element_type=F32)


def _dot_nt(a, b):
    return lax.dot_general(a, b, (((1,), (1,)), ((), ())), preferred_element_type=F32)


def _dot_tn(a, b):
    return lax.dot_general(a, b, (((0,), (0,)), ((), ())), preferred_element_type=F32)


def _sigmoid(x):
    return 1.0 / (1.0 + jnp.exp(-x))


def _rmsnorm_kernel(h_ref, g_ref, o_ref):
    x = h_ref[...]
    ms = jnp.mean(x * x, axis=-1, keepdims=True)
    o_ref[...] = (x * lax.rsqrt(ms + NORM_EPS) * g_ref[...]).astype(o_ref.dtype)


def rmsnorm(h, g, out_dtype, tm=256):
    m, d = h.shape
    return pl.pallas_call(
        _rmsnorm_kernel,
        grid=(m // tm,),
        in_specs=[pl.BlockSpec((tm, d), lambda i: (i, 0)),
                  pl.BlockSpec((1, d), lambda i: (0, 0))],
        out_specs=pl.BlockSpec((tm, d), lambda i: (i, 0)),
        out_shape=jax.ShapeDtypeStruct((m, d), out_dtype),
        compiler_params=_cparams(("parallel",)),
    )(h, g.reshape(1, d))


def _mm_kernel(a_ref, w_ref, o_ref):
    o_ref[...] = _dot(a_ref[...], w_ref[...]).astype(o_ref.dtype)


def matmul(a, w, out_dtype, tm, tn):
    m, k = a.shape
    n = w.shape[1]
    tn = min(tn, n)
    return pl.pallas_call(
        _mm_kernel,
        grid=(m // tm, n // tn),
        in_specs=[pl.BlockSpec((tm, k), lambda i, j: (i, 0)),
                  pl.BlockSpec((k, tn), lambda i, j: (0, j))],
        out_specs=pl.BlockSpec((tm, tn), lambda i, j: (i, j)),
        out_shape=jax.ShapeDtypeStruct((m, n), out_dtype),
        compiler_params=_cparams(("parallel", "arbitrary")),
    )(a, w)


def _mm_res_kernel(a_ref, w_ref, r_ref, o_ref):
    o_ref[...] = r_ref[...] + _dot(a_ref[...], w_ref[...])


def matmul_residual(a, w, res, tm, tn):
    m, k = a.shape
    n = w.shape[1]
    tn = min(tn, n)
    return pl.pallas_call(
        _mm_res_kernel,
        grid=(m // tm, n // tn),
        in_specs=[pl.BlockSpec((tm, k), lambda i, j: (i, 0)),
                  pl.BlockSpec((k, tn), lambda i, j: (0, j)),
                  pl.BlockSpec((tm, tn), lambda i, j: (i, j))],
        out_specs=pl.BlockSpec((tm, tn), lambda i, j: (i, j)),
        out_shape=jax.ShapeDtypeStruct((m, n), F32),
        compiler_params=_cparams(("parallel", "arbitrary")),
    )(a, w, res)


def _mix_out_kernel(a1_ref, a2_ref, a3_ref, w1_ref, w2_ref, w3_ref, r_ref, o_ref):
    acc = _dot(a1_ref[...], w1_ref[...])
    acc = acc + _dot(a2_ref[...], w2_ref[...])
    acc = acc + _dot(a3_ref[...], w3_ref[...])
    o_ref[...] = r_ref[...] + acc


def mix_out_matmul(a1, a2, a3, w1, w2, w3, res, tm, tn):
    m = a1.shape[0]
    n = w1.shape[1]
    tn = min(tn, n)
    k1, k2, k3 = a1.shape[1], a2.shape[1], a3.shape[1]
    return pl.pallas_call(
        _mix_out_kernel,
        grid=(m // tm, n // tn),
        in_specs=[pl.BlockSpec((tm, k1), lambda i, j: (i, 0)),
                  pl.BlockSpec((tm, k2), lambda i, j: (i, 0)),
                  pl.BlockSpec((tm, k3), lambda i, j: (i, 0)),
                  pl.BlockSpec((k1, tn), lambda i, j: (0, j)),
                  pl.BlockSpec((k2, tn), lambda i, j: (0, j)),
                  pl.BlockSpec((k3, tn), lambda i, j: (0, j)),
                  pl.BlockSpec((tm, tn), lambda i, j: (i, j))],
        out_specs=pl.BlockSpec((tm, tn), lambda i, j: (i, j)),
        out_shape=jax.ShapeDtypeStruct((m, n), F32),
        compiler_params=_cparams(("parallel", "arbitrary")),
    )(a1, a2, a3, w1, w2, w3, res)


def _swiglu_kernel(a_ref, wg_ref, wu_ref, o_ref):
    a = a_ref[...]
    g = _dot(a, wg_ref[...])
    u = _dot(a, wu_ref[...])
    o_ref[...] = (g * _sigmoid(g) * u).astype(o_ref.dtype)


def swiglu_matmul(a, wg, wu, tm, tn):
    m, k = a.shape
    n = wg.shape[1]
    tn = min(tn, n)
    return pl.pallas_call(
        _swiglu_kernel,
        grid=(m // tm, n // tn),
        in_specs=[pl.BlockSpec((tm, k), lambda i, j: (i, 0)),
                  pl.BlockSpec((k, tn), lambda i, j: (0, j)),
                  pl.BlockSpec((k, tn), lambda i, j: (0, j))],
        out_specs=pl.BlockSpec((tm, tn), lambda i, j: (i, j)),
        out_shape=jax.ShapeDtypeStruct((m, n), BF16),
        compiler_params=_cparams(("parallel", "arbitrary")),
    )(a, wg, wu)


def _conv_kernel(cb_ref, cc_ref, ch_ref, w_ref, o_ref, carry_ref):
    @pl.when(pl.program_id(1) == 0)
    def _():
        carry_ref[...] = jnp.zeros_like(carry_ref)

    u = cc_ref[...] * ch_ref[...]
    tt = u.shape[0]
    row = lax.broadcasted_iota(jnp.int32, u.shape, 0)
    c0 = carry_ref[0:1, :]
    c1 = carry_ref[1:2, :]
    u1 = jnp.where(row == 0, c1, pltpu.roll(u, 1, axis=0))
    u2 = jnp.where(row == 0, c0, jnp.where(row == 1, c1, pltpu.roll(u, 2, axis=0)))
    carry_ref[0:2, :] = u[tt - 2:tt, :]
    y = u2 * w_ref[0:1, :]
    y = y + u1 * w_ref[1:2, :]
    y = y + u * w_ref[2:3, :]
    o_ref[...] = (cb_ref[...] * y).astype(o_ref.dtype)


def conv_mixer(p, conv_w, batch, tp, width, tt):
    nt = tp // tt
    w = jnp.zeros((8, width), F32).at[0:CONV_K].set(conv_w.T)
    return pl.pallas_call(
        _conv_kernel,
        grid=(batch, nt),
        in_specs=[pl.BlockSpec((tt, width), lambda b, t: (b * nt + t, 0)),
                  pl.BlockSpec((tt, width), lambda b, t: (b * nt + t, 1)),
                  pl.BlockSpec((tt, width), lambda b, t: (b * nt + t, 2)),
                  pl.BlockSpec((8, width), lambda b, t: (0, 0))],
        out_specs=pl.BlockSpec((tt, width), lambda b, t: (b * nt + t, 0)),
        out_shape=jax.ShapeDtypeStruct((batch * tp, width), BF16),
        scratch_shapes=[pltpu.VMEM((8, width), F32)],
        compiler_params=_cparams(("parallel", "arbitrary")),
    )(p, p, p, w)


def _swa_kernel(sink_ref, q_ref, kp_ref, kc_ref, km_ref, vp_ref, vc_ref, vm_ref, o_ref, *, slopes):
    j = pl.program_id(1)
    blk = ROW_BLOCK
    g = ATTN_GROUP
    scale = HEAD_DIM ** -0.5
    rows = g * blk
    ri = lax.broadcasted_iota(jnp.int32, (rows, 3 * blk), 0)
    ci = lax.broadcasted_iota(jnp.int32, (rows, 3 * blk), 1)
    q_pos = j * blk + ri % blk
    k_pos = jnp.where(ci < blk, ci, (j - 2) * blk + ci)
    dist = q_pos - k_pos
    is_meta = ci < blk
    ok = ((ci < N_META) & (ci <= q_pos)) | (
        (ci >= blk) & (dist >= 0) & (dist < WINDOW) & (k_pos >= N_META))
    distf = jnp.where(is_meta, 0, dist).astype(F32)
    rgrp = lax.broadcasted_iota(jnp.int32, (rows, 1), 0) // blk
    for h in range(ATTN_KV_HEADS):
        sl = slice(h * HEAD_DIM, (h + 1) * HEAD_DIM)
        kb = jnp.concatenate([km_ref[:, sl], kp_ref[:, sl], kc_ref[:, sl]], axis=0).astype(BF16)
        vb = jnp.concatenate([vm_ref[:, sl], vp_ref[:, sl], vc_ref[:, sl]], axis=0).astype(BF16)
        q3 = jnp.concatenate(
            [q_ref[:, (h * g + i) * HEAD_DIM:(h * g + i + 1) * HEAD_DIM] for i in range(g)],
            axis=0).astype(BF16)
        slope = jnp.full((rows, 1), slopes[h * g], F32)
        sink = jnp.full((rows, 1), sink_ref[h * g], F32)
        for i in range(1, g):
            slope = jnp.where(rgrp == i, slopes[h * g + i], slope)
            sink = jnp.where(rgrp == i, sink_ref[h * g + i], sink)
        s = _dot_nt(q3, kb) * scale - slope * distf
        s = jnp.where(ok, s, NEG)
        mx = jnp.maximum(jnp.max(s, axis=-1, keepdims=True), sink)
        e = jnp.exp(s - mx)
        den = jnp.sum(e, axis=-1, keepdims=True) + jnp.exp(sink - mx)
        o = _dot(e.astype(BF16), vb) / den
        for i in range(g):
            o_ref[:, (h * g + i) * HEAD_DIM:(h * g + i + 1) * HEAD_DIM] = (
                o[i * blk:(i + 1) * blk, :].astype(o_ref.dtype))


def swa_mixer(p, sinks, batch, tp, col0):
    nb = tp // ROW_BLOCK
    qw = ATTN_HEADS * HEAD_DIM
    kw = ATTN_KV_HEADS * HEAD_DIM
    assert col0 % qw == 0 and (col0 + qw) % kw == 0
    qc = col0 // qw
    kc = (col0 + qw) // kw
    vc = kc + 1
    cur = lambda c: (lambda b, j, s: (b * nb + j, c))
    prev = lambda c: (lambda b, j, s: (b * nb + jnp.maximum(j - 1, 0), c))
    meta = lambda c: (lambda b, j, s: (b * nb, c))
    grid_spec = pltpu.PrefetchScalarGridSpec(
        num_scalar_prefetch=1,
        grid=(batch, nb),
        in_specs=[pl.BlockSpec((ROW_BLOCK, qw), cur(qc)),
                  pl.BlockSpec((ROW_BLOCK, kw), prev(kc)),
                  pl.BlockSpec((ROW_BLOCK, kw), cur(kc)),
                  pl.BlockSpec((ROW_BLOCK, kw), meta(kc)),
                  pl.BlockSpec((ROW_BLOCK, kw), prev(vc)),
                  pl.BlockSpec((ROW_BLOCK, kw), cur(vc)),
                  pl.BlockSpec((ROW_BLOCK, kw), meta(vc))],
        out_specs=pl.BlockSpec((ROW_BLOCK, qw), lambda b, j, s: (b * nb + j, 0)),
    )
    return pl.pallas_call(
        functools.partial(_swa_kernel, slopes=_alibi_slopes(ATTN_HEADS)),
        grid_spec=grid_spec,
        out_shape=jax.ShapeDtypeStruct((batch * tp, qw), BF16),
        compiler_params=_cparams(("parallel", "arbitrary")),
    )(sinks.astype(F32), p, p, p, p, p, p, p)


def _split3(x):
    p1 = x.astype(BF16)
    r1 = x - p1.astype(F32)
    p2 = r1.astype(BF16)
    p3 = (r1 - p2.astype(F32)).astype(BF16)
    return p1, p2, p3


def _rwkv_kernel(pr_ref, pk_ref, pv_ref, pl_ref, par_ref, mul_ref, w2_ref, a2_ref, g2_ref,
                 seg_ref, o_ref, s_ref, y_ref, cr_ref, ck_ref, cv_ref, cl_ref):
    tb, lg = pr_ref.shape
    npair = lg // LANES
    c = CHUNK

    @pl.when(pl.program_id(2) == 0)
    def _():
        s_ref[...] = jnp.zeros_like(s_ref)
        cr_ref[...] = jnp.zeros_like(cr_ref)
        ck_ref[...] = jnp.zeros_like(ck_ref)
        cv_ref[...] = jnp.zeros_like(cv_ref)
        cl_ref[...] = jnp.zeros_like(cl_ref)

    def shift_mix(x_ref, c_ref, mu):
        x = x_ref[...]
        row = lax.broadcasted_iota(jnp.int32, x.shape, 0)
        prev = jnp.where(row == 0, c_ref[0:1, :], pltpu.roll(x, 1, axis=0))
        c_ref[0:1, :] = x[tb - 1:tb, :]
        return x + (prev - x) * mu

    par = par_ref[...]
    mu_r, mu_k, mu_v = par[0:1], par[1:2], par[2:3]
    w0, a0, k_k, k_a, r_k = par[3:4], par[4:5], par[5:6], par[6:7], par[7:8]
    ln_w, ln_b = par[8:9], par[9:10]

    r = shift_mix(pr_ref, cr_ref, mu_r)
    k = shift_mix(pk_ref, ck_ref, mu_k)
    v = shift_mix(pv_ref, cv_ref, mu_v)
    lo = shift_mix(pl_ref, cl_ref, mul_ref[...])
    wl = lo[:, 0:DECAY_RANK]
    al = lo[:, DECAY_RANK:DECAY_RANK + ICLR_RANK]
    gl = lo[:, DECAY_RANK + ICLR_RANK:DECAY_RANK + ICLR_RANK + g2_ref.shape[0]]

    seg = seg_ref[...]

    def seg_sum(x):
        hi = x.astype(BF16)
        lo_ = (x - hi.astype(F32)).astype(BF16)
        return _dot(hi, seg) + _dot(lo_, seg)

    z = -(w0 + _dot(jnp.tanh(wl).astype(BF16), w2_ref[...]))
    w = -(jnp.maximum(z, 0.0) + jnp.log(1.0 + jnp.exp(-jnp.abs(z)))) - 0.5
    lw = -jnp.exp(w)
    a = _sigmoid(a0 + _dot(al.astype(BF16), a2_ref[...]))
    gate = _dot(_sigmoid(gl).astype(BF16), g2_ref[...])

    kk = k * k_k
    kk = kk / jnp.maximum(jnp.sqrt(seg_sum(kk * kk)), 1e-12)
    k2 = k * (1.0 + (a - 1.0) * k_a)
    bvec = kk * a
    bonus = seg_sum(r * k2 * r_k) * v

    ti = lax.broadcasted_iota(jnp.int32, (tb, tb), 0)
    tj = lax.broadcasted_iota(jnp.int32, (tb, tb), 1)
    tri = jnp.where((ti // c == tj // c) & (tj <= ti), 1.0, 0.0).astype(BF16)
    l1, l2, l3 = _split3(lw)
    cum = _dot(tri, l1) + _dot(tri, l2) + _dot(tri, l3)

    r_hat = r * jnp.exp(cum)
    a_hat = -kk * jnp.exp(cum - lw)
    dinv = jnp.exp(-cum)
    b_til = bvec * dinv
    k_til = k2 * dinv

    lane = lax.broadcasted_iota(jnp.int32, (c, LANES), 1)
    m0 = lane < RWKV_N

    def stack(x):
        return jnp.concatenate([jnp.where(m0, x, 0.0), jnp.where(m0, 0.0, x)], axis=0)

    ri = lax.broadcasted_iota(jnp.int32, (2 * c, 2 * c), 0)
    ci = lax.broadcasted_iota(jnp.int32, (2 * c, 2 * c), 1)
    same = (ri // c) == (ci // c)
    strict = same & (ci < ri)
    incl = same & (ci <= ri)
    eye = jnp.where(ri == ci, 1.0, 0.0)

    for ch in range(tb // c):
        rs = slice(ch * c, (ch + 1) * c)
        last = ch * c + c - 1
        for pr in range(npair):
            ls = slice(pr * LANES, (pr + 1) * LANES)
            cum_c = cum[rs, ls]
            cum_last = cum[last:last + 1, ls]
            dec_rest = jnp.exp(cum_last - cum_c)
            d_tot = jnp.exp(cum_last)
            lhs4 = jnp.concatenate([stack(a_hat[rs, ls]), stack(r_hat[rs, ls])], axis=0).astype(BF16)
            bt = b_til[rs, ls]
            kt = k_til[rs, ls]
            rhs4 = jnp.concatenate([bt, bt, kt, kt], axis=0).astype(BF16)
            sc = _dot_nt(lhs4, rhs4)
            l_ab = jnp.where(strict, sc[0:2 * c, 0:2 * c], 0.0)
            m_ak = jnp.where(strict, sc[0:2 * c, 2 * c:4 * c], 0.0)
            m_rb = jnp.where(incl, sc[2 * c:4 * c, 0:2 * c], 0.0)
            m_rk = jnp.where(incl, sc[2 * c:4 * c, 2 * c:4 * c], 0.0)
            tinv = eye + l_ab
            lp = l_ab
            for _ in range(int(math.log2(c)) - 1):
                lpb = lp.astype(BF16)
                lp = _dot(lpb, lpb)
                tinv = tinv + _dot(tinv.astype(BF16), lp.astype(BF16))

            vs = stack(v[rs, ls]).astype(BF16)
            state = s_ref[pr]
            xs = _dot_nt(lhs4, state.astype(BF16))
            rhs = xs[0:2 * c] + _dot(m_ak.astype(BF16), vs)
            us = _dot(tinv.astype(BF16), rhs.astype(BF16)).astype(BF16)
            ys = xs[2 * c:4 * c] + _dot(m_rb.astype(BF16), us) + _dot(m_rk.astype(BF16), vs)
            y_ref[rs, ls] = ys[0:c] + ys[c:2 * c]
            uv = jnp.concatenate([us, vs], axis=0)
            bk = jnp.concatenate([stack(bvec[rs, ls] * dec_rest), stack(k2[rs, ls] * dec_rest)],
                                 axis=0).astype(BF16)
            s_ref[pr] = state * d_tot + _dot_tn(uv, bk)

    y = y_ref[...]
    inv_n = 1.0 / RWKV_N
    mean = seg_sum(y) * inv_n
    yc = y - mean
    var = seg_sum(yc * yc) * inv_n
    yn = yc * lax.rsqrt(var + GN_EPS) * ln_w + ln_b
    o_ref[...] = ((yn + bonus) * gate).astype(o_ref.dtype)


def rwkv_mixer(p, par, mu_lora, w2, a2, g2, batch, tp, col0, width, lg):
    nb = tp // ROW_BLOCK
    ng = width // lg
    lw = mu_lora.shape[1]
    assert col0 % lg == 0 and width % lg == 0 and (col0 + 3 * width) % lw == 0
    c_r = col0 // lg
    c_k = c_r + ng
    c_v = c_k + ng
    c_l = (col0 + 3 * width) // lw
    seg = (np.arange(lg)[:, None] // RWKV_N == np.arange(lg)[None, :] // RWKV_N)
    seg = jnp.asarray(seg, BF16)
    row = lambda b, g, t: b * nb + t
    return pl.pallas_call(
        _rwkv_kernel,
        grid=(batch, ng, nb),
        in_specs=[pl.BlockSpec((ROW_BLOCK, lg), lambda b, g, t: (row(b, g, t), c_r + g)),
                  pl.BlockSpec((ROW_BLOCK, lg), lambda b, g, t: (row(b, g, t), c_k + g)),
                  pl.BlockSpec((ROW_BLOCK, lg), lambda b, g, t: (row(b, g, t), c_v + g)),
                  pl.BlockSpec((ROW_BLOCK, lw), lambda b, g, t: (row(b, g, t), c_l)),
                  pl.BlockSpec((16, lg), lambda b, g, t: (0, g)),
                  pl.BlockSpec((1, lw), lambda b, g, t: (0, 0)),
                  pl.BlockSpec((DECAY_RANK, lg), lambda b, g, t: (0, g)),
                  pl.BlockSpec((ICLR_RANK, lg), lambda b, g, t: (0, g)),
                  pl.BlockSpec((g2.shape[0], lg), lambda b, g, t: (0, g)),
                  pl.BlockSpec((lg, lg), lambda b, g, t: (0, 0))],
        out_specs=pl.BlockSpec((ROW_BLOCK, lg), lambda b, g, t: (row(b, g, t), g)),
        out_shape=jax.ShapeDtypeStruct((batch * tp, width), BF16),
        scratch_shapes=[pltpu.VMEM((lg // LANES, LANES, LANES), F32),
                        pltpu.VMEM((ROW_BLOCK, lg), F32),
                        pltpu.VMEM((8, lg), F32), pltpu.VMEM((8, lg), F32),
                        pltpu.VMEM((8, lg), F32), pltpu.VMEM((8, lw), F32)],
        compiler_params=_cparams(("parallel", "parallel", "arbitrary")),
    )(p, p, p, p, par, mu_lora, w2, a2, g2, seg)


def _pad_cols(w, n):
    return jnp.pad(w, ((0, 0), (0, n - w.shape[1])))


def _pad_rows(w, n):
    return jnp.pad(w, ((0, n - w.shape[0]), (0, 0)))


def kernel(x, meta_tokens, norm_mix, w_in, conv_w, attn_sinks, rwkv_mu, rwkv_w0, rwkv_w2, rwkv_a0, rwkv_a2, rwkv_g2, rwkv_k_k, rwkv_k_a, rwkv_r_k, rwkv_ln_w, rwkv_ln_b, w_out, norm_ffn, w_ffn_in, w_ffn_out, norm_final):
    batch, seq, d = x.shape
    depth = w_in.shape[0]
    conv_width = conv_w.shape[1]
    attn_width = ATTN_HEADS * HEAD_DIM
    attn_cols = attn_width + 2 * ATTN_KV_HEADS * HEAD_DIM
    rwkv_width = rwkv_w0.shape[1]
    ffn_hidden = w_ffn_out.shape[1]
    t_real = N_META + seq
    tp = -(-t_real // ROW_BLOCK) * ROW_BLOCK
    m = batch * tp

    lora_w = 1024
    gate_pad = 512
    col_attn = CONV_K * conv_width
    col_rwkv = col_attn + attn_cols
    in_cols = col_rwkv + 3 * rwkv_width + lora_w
    hid_pad = -(-ffn_hidden // 1024) * 1024
    lg = 512

    tm = m // 13 if m % 13 == 0 else ROW_BLOCK
    tm_half = tm // 2 if tm % 16 == 0 else tm

    meta = jnp.broadcast_to(meta_tokens.astype(x.dtype)[None], (batch, N_META, d))
    h = jnp.concatenate([meta, x, jnp.zeros((batch, tp - t_real, d), x.dtype)], axis=1)
    h = h.reshape(m, d)

    for l in range(depth):
        w_in_l = _pad_cols(w_in[l], in_cols).astype(BF16)
        mu = rwkv_mu[l]
        par = jnp.zeros((16, rwkv_width), F32)
        par = par.at[0].set(mu[0:rwkv_width])
        par = par.at[1].set(mu[rwkv_width:2 * rwkv_width])
        par = par.at[2].set(mu[2 * rwkv_width:3 * rwkv_width])
        par = par.at[3].set(rwkv_w0[l]).at[4].set(rwkv_a0[l]).at[5].set(rwkv_k_k[l])
        par = par.at[6].set(rwkv_k_a[l]).at[7].set(rwkv_r_k[l])
        par = par.at[8].set(rwkv_ln_w[l]).at[9].set(rwkv_ln_b[l])
        mu_lora = _pad_cols(mu[3 * rwkv_width:].reshape(1, -1), lora_w)
        g2 = _pad_rows(rwkv_g2[l], gate_pad).astype(BF16)

        u = rmsnorm(h, norm_mix[l], BF16)
        p = matmul(u, w_in_l, F32, tm, 512)
        y_conv = conv_mixer(p, conv_w[l], batch, tp, conv_width, tp // 13 if tp % 13 == 0 else ROW_BLOCK)
        y_attn = swa_mixer(p, attn_sinks[l], batch, tp, col_attn)
        y_rwkv = rwkv_mixer(p, par, mu_lora, rwkv_w2[l].astype(BF16), rwkv_a2[l].astype(BF16), g2,
                            batch, tp, col_rwkv, rwkv_width, lg)
        wo = w_out[l].astype(BF16)
        h = mix_out_matmul(y_conv, y_attn, y_rwkv,
                           wo[0:conv_width], wo[conv_width:conv_width + attn_width],
                           wo[conv_width + attn_width:], h, tm, 512)

        u = rmsnorm(h, norm_ffn[l], BF16)
        wg = _pad_cols(w_ffn_in[l][:, 0:ffn_hidden], hid_pad).astype(BF16)
        wu = _pad_cols(w_ffn_in[l][:, ffn_hidden:], hid_pad).astype(BF16)
        act = swiglu_matmul(u, wg, wu, tm, 512)
        wd = _pad_rows(w_ffn_out[l], hid_pad).astype(BF16)
        h = matmul_residual(act, wd, h, tm_half, 256)

    out = rmsnorm(h, norm_final, x.dtype)
    return out.reshape(batch, tp, d)[:, N_META:t_real]
```

```python
import functools
import math

import numpy as np
import jax
import jax.numpy as jnp
from jax import lax
from jax.experimental import pallas as pl
from jax.experimental.pallas import tpu as pltpu

F32 = jnp.float32
BF16 = jnp.bfloat16

N_META = 16
NORM_EPS = 1e-5
GN_EPS = 64e-5
CONV_K = 3
HEAD_DIM = 128
ATTN_HEADS = 12
ATTN_KV_HEADS = 4
ATTN_GROUP = ATTN_HEADS // ATTN_KV_HEADS
WINDOW = 128
RWKV_N = 64
DECAY_RANK = 128
ICLR_RANK = 128
GATE_RANK = 480

LANES = 128
ROW_BLOCK = 128
CHUNK = 64
VMEM_LIMIT = 56 * 1024 * 1024
NEG = -1e30


def _alibi_slopes(n):
    def pow2_slopes(m):
        start = 2.0 ** (-8.0 / m)
        return [start ** (i + 1) for i in range(m)]
    c = 2 ** int(math.floor(math.log2(n)))
    s = pow2_slopes(c)
    if c < n:
        s = s + pow2_slopes(2 * c)[0::2][: n - c]
    return [float(np.float32(v)) for v in s]


def _cparams(sem):
    return pltpu.CompilerParams(dimension_semantics=sem, vmem_limit_bytes=VMEM_LIMIT)


def _dot(a, b):
    return jnp.dot(a, b, preferred_element_type=F32)


def _dot_nt(a, b):
    return lax.dot_general(a, b, (((1,), (1,)), ((), ())), preferred_element_type=F32)


def _dot_tn(a, b):
    return lax.dot_general(a, b, (((0,), (0,)), ((), ())), preferred_element_type=F32)


def _sigmoid(x):
    return 1.0 / (1.0 + jnp.exp(-x))


def _rms_scale(x, g):
    ms = jnp.mean(x * x, axis=-1, keepdims=True)
    return x * lax.rsqrt(ms + NORM_EPS) * g


def _rmsnorm_kernel(h_ref, g_ref, o_ref):
    o_ref[...] = _rms_scale(h_ref[...], g_ref[...]).astype(o_ref.dtype)


def rmsnorm(h, g, out_dtype, tm=256):
    m, d = h.shape
    return pl.pallas_call(
        _rmsnorm_kernel,
        grid=(m // tm,),
        in_specs=[pl.BlockSpec((tm, d), lambda i: (i, 0)),
                  pl.BlockSpec((1, d), lambda i: (0, 0))],
        out_specs=pl.BlockSpec((tm, d), lambda i: (i, 0)),
        out_shape=jax.ShapeDtypeStruct((m, d), out_dtype),
        compiler_params=_cparams(("parallel",)),
    )(h, g.reshape(1, d))


def _embed_norm_kernel(x_ref, meta_ref, g_ref, h_ref, u_ref):
    pad = h_ref.shape[0] - meta_ref.shape[0]

    @pl.when(pl.program_id(1) == 0)
    def _():
        h_ref[0:pad, :] = jnp.zeros((pad, h_ref.shape[1]), h_ref.dtype)
        h_ref[pad:, :] = meta_ref[...]

    @pl.when(pl.program_id(1) > 0)
    def _():
        h_ref[...] = x_ref[...]

    u_ref[...] = _rms_scale(h_ref[...], g_ref[...]).astype(u_ref.dtype)


def embed_norm(x2, meta_tokens, g, batch, nb):
    d = x2.shape[1]
    nx = nb - 1
    blk = pl.BlockSpec((ROW_BLOCK, d), lambda b, i: (b * nb + i, 0))
    return pl.pallas_call(
        _embed_norm_kernel,
        grid=(batch, nb),
        in_specs=[pl.BlockSpec((ROW_BLOCK, d), lambda b, i: (b * nx + jnp.maximum(i - 1, 0), 0)),
                  pl.BlockSpec((N_META, d), lambda b, i: (0, 0)),
                  pl.BlockSpec((1, d), lambda b, i: (0, 0))],
        out_specs=[blk, blk],
        out_shape=[jax.ShapeDtypeStruct((batch * nb * ROW_BLOCK, d), F32),
                   jax.ShapeDtypeStruct((batch * nb * ROW_BLOCK, d), BF16)],
        compiler_params=_cparams(("parallel", "arbitrary")),
    )(x2, meta_tokens, g.reshape(1, d))


def final_norm(h, g, batch, nb, out_dtype):
    d = h.shape[1]
    nx = nb - 1
    return pl.pallas_call(
        _rmsnorm_kernel,
        grid=(batch, nx),
        in_specs=[pl.BlockSpec((ROW_BLOCK, d), lambda b, i: (b * nb + i + 1, 0)),
                  pl.BlockSpec((1, d), lambda b, i: (0, 0))],
        out_specs=pl.BlockSpec((ROW_BLOCK, d), lambda b, i: (b * nx + i, 0)),
        out_shape=jax.ShapeDtypeStruct((batch * nx * ROW_BLOCK, d), out_dtype),
        compiler_params=_cparams(("parallel", "parallel")),
    )(h, g.reshape(1, d))


def _mm_kernel(a_ref, w_ref, o_ref):
    o_ref[...] = _dot(a_ref[...], w_ref[...]).astype(o_ref.dtype)


def matmul(a, w, out_dtype, tm, tn):
    m, k = a.shape
    n = w.shape[1]
    tn = min(tn, n)
    return pl.pallas_call(
        _mm_kernel,
        grid=(m // tm, n // tn),
        in_specs=[pl.BlockSpec((tm, k), lambda i, j: (i, 0)),
                  pl.BlockSpec((k, tn), lambda i, j: (0, j))],
        out_specs=pl.BlockSpec((tm, tn), lambda i, j: (i, j)),
        out_shape=jax.ShapeDtypeStruct((m, n), out_dtype),
        compiler_params=_cparams(("parallel", "arbitrary")),
    )(a, w)


def _mm_res_kernel(a_ref, w_ref, r_ref, o_ref):
    o_ref[...] = r_ref[...] + _dot(a_ref[...], w_ref[...])


def matmul_residual(a, w, res, tm, tn):
    m, k = a.shape
    n = w.shape[1]
    tn = min(tn, n)
    return pl.pallas_call(
        _mm_res_kernel,
        grid=(m // tm, n // tn),
        in_specs=[pl.BlockSpec((tm, k), lambda i, j: (i, 0)),
                  pl.BlockSpec((k, tn), lambda i, j: (0, j)),
                  pl.BlockSpec((tm, tn), lambda i, j: (i, j))],
        out_specs=pl.BlockSpec((tm, tn), lambda i, j: (i, j)),
        out_shape=jax.ShapeDtypeStruct((m, n), F32),
        compiler_params=_cparams(("parallel", "arbitrary")),
    )(a, w, res)


def _mix_out_kernel(a1_ref, a2_ref, a3_ref, w1_ref, w2_ref, w3_ref, r_ref, o_ref):
    acc = _dot(a1_ref[...], w1_ref[...])
    acc = acc + _dot(a2_ref[...], w2_ref[...])
    acc = acc + _dot(a3_ref[...], w3_ref[...])
    o_ref[...] = r_ref[...] + acc


def mix_out_matmul(a1, a2, a3, w1, w2, w3, res, tm, tn):
    m = a1.shape[0]
    n = w1.shape[1]
    tn = min(tn, n)
    k1, k2, k3 = a1.shape[1], a2.shape[1], a3.shape[1]
    return pl.pallas_call(
        _mix_out_kernel,
        grid=(m // tm, n // tn),
        in_specs=[pl.BlockSpec((tm, k1), lambda i, j: (i, 0)),
                  pl.BlockSpec((tm, k2), lambda i, j: (i, 0)),
                  pl.BlockSpec((tm, k3), lambda i, j: (i, 0)),
                  pl.BlockSpec((k1, tn), lambda i, j: (0, j)),
                  pl.BlockSpec((k2, tn), lambda i, j: (0, j)),
                  pl.BlockSpec((k3, tn), lambda i, j: (0, j)),
                  pl.BlockSpec((tm, tn), lambda i, j: (i, j))],
        out_specs=pl.BlockSpec((tm, tn), lambda i, j: (i, j)),
        out_shape=jax.ShapeDtypeStruct((m, n), F32),
        compiler_params=_cparams(("parallel", "arbitrary")),
    )(a1, a2, a3, w1, w2, w3, res)


def _swiglu_kernel(a_ref, wg_ref, wu_ref, o_ref):
    a = a_ref[...]
    g = _dot(a, wg_ref[...])
    u = _dot(a, wu_ref[...])
    o_ref[...] = (g * _sigmoid(g) * u).astype(o_ref.dtype)


def swiglu_matmul(a, wg, wu, tm, tn):
    m, k = a.shape
    n = wg.shape[1]
    tn = min(tn, n)
    return pl.pallas_call(
        _swiglu_kernel,
        grid=(m // tm, n // tn),
        in_specs=[pl.BlockSpec((tm, k), lambda i, j: (i, 0)),
                  pl.BlockSpec((k, tn), lambda i, j: (0, j)),
                  pl.BlockSpec((k, tn), lambda i, j: (0, j))],
        out_specs=pl.BlockSpec((tm, tn), lambda i, j: (i, j)),
        out_shape=jax.ShapeDtypeStruct((m, n), BF16),
        compiler_params=_cparams(("parallel", "arbitrary")),
    )(a, wg, wu)


def _conv_kernel(cb_ref, cc_ref, ch_ref, w_ref, o_ref, carry_ref, *, pad):
    first = pl.program_id(1) == 0

    @pl.when(first)
    def _():
        carry_ref[...] = jnp.zeros_like(carry_ref)

    u = cc_ref[...] * ch_ref[...]
    tt = u.shape[0]
    row = lax.broadcasted_iota(jnp.int32, u.shape, 0)
    u = jnp.where((row >= pad) | jnp.logical_not(first), u, 0.0)
    c0 = carry_ref[0:1, :]
    c1 = carry_ref[1:2, :]
    u1 = jnp.where(row == 0, c1, pltpu.roll(u, 1, axis=0))
    u2 = jnp.where(row == 0, c0, jnp.where(row == 1, c1, pltpu.roll(u, 2, axis=0)))
    carry_ref[0:2, :] = u[tt - 2:tt, :]
    y = u2 * w_ref[0:1, :]
    y = y + u1 * w_ref[1:2, :]
    y = y + u * w_ref[2:3, :]
    o_ref[...] = (cb_ref[...] * y).astype(o_ref.dtype)


def conv_mixer(p, conv_w, batch, tp, pad, col0, width, tt):
    nt = tp // tt
    assert pad <= tt and col0 % width == 0
    cb = col0 // width
    w = jnp.zeros((8, width), F32).at[0:CONV_K].set(conv_w.T)
    return pl.pallas_call(
        functools.partial(_conv_kernel, pad=pad),
        grid=(batch, nt),
        in_specs=[pl.BlockSpec((tt, width), lambda b, t: (b * nt + t, cb)),
                  pl.BlockSpec((tt, width), lambda b, t: (b * nt + t, cb + 1)),
                  pl.BlockSpec((tt, width), lambda b, t: (b * nt + t, cb + 2)),
                  pl.BlockSpec((8, width), lambda b, t: (0, 0))],
        out_specs=pl.BlockSpec((tt, width), lambda b, t: (b * nt + t, 0)),
        out_shape=jax.ShapeDtypeStruct((batch * tp, width), BF16),
        scratch_shapes=[pltpu.VMEM((8, width), F32)],
        compiler_params=_cparams(("parallel", "arbitrary")),
    )(p, p, p, w)


def _swa_kernel(sink_ref, q_ref, kp_ref, kc_ref, km_ref, vp_ref, vc_ref, vm_ref, o_ref, *,
                slopes, pad):
    j = pl.program_id(1)
    blk = ROW_BLOCK
    g = ATTN_GROUP
    scale = HEAD_DIM ** -0.5
    rows = g * blk
    ri = lax.broadcasted_iota(jnp.int32, (rows, 3 * blk), 0)
    ci = lax.broadcasted_iota(jnp.int32, (rows, 3 * blk), 1)
    q_pos = j * blk + ri % blk - pad
    k_pos = jnp.where(ci < blk, ci, (j - 2) * blk + ci) - pad
    dist = q_pos - k_pos
    is_meta = ci < blk
    ok = (is_meta & (k_pos >= 0) & (k_pos <= q_pos)) | (
        (ci >= blk) & (dist >= 0) & (dist < WINDOW) & (k_pos >= N_META))
    distf = jnp.where(is_meta, 0, dist).astype(F32)
    rgrp = lax.broadcasted_iota(jnp.int32, (rows, 1), 0) // blk
    for h in range(ATTN_KV_HEADS):
        sl = slice(h * HEAD_DIM, (h + 1) * HEAD_DIM)
        kb = jnp.concatenate([km_ref[:, sl], kp_ref[:, sl], kc_ref[:, sl]], axis=0).astype(BF16)
        vb = jnp.concatenate([vm_ref[:, sl], vp_ref[:, sl], vc_ref[:, sl]], axis=0).astype(BF16)
        q3 = jnp.concatenate(
            [q_ref[:, (h * g + i) * HEAD_DIM:(h * g + i + 1) * HEAD_DIM] for i in range(g)],
            axis=0).astype(BF16)
        slope = jnp.full((rows, 1), slopes[h * g], F32)
        sink = jnp.full((rows, 1), sink_ref[h * g], F32)
        for i in range(1, g):
            slope = jnp.where(rgrp == i, slopes[h * g + i], slope)
            sink = jnp.where(rgrp == i, sink_ref[h * g + i], sink)
        s = _dot_nt(q3, kb) * scale - slope * distf
        s = jnp.where(ok, s, NEG)
        mx = jnp.maximum(jnp.max(s, axis=-1, keepdims=True), sink)
        e = jnp.exp(s - mx)
        den = jnp.sum(e, axis=-1, keepdims=True) + jnp.exp(sink - mx)
        o = _dot(e.astype(BF16), vb) / den
        for i in range(g):
            o_ref[:, (h * g + i) * HEAD_DIM:(h * g + i + 1) * HEAD_DIM] = (
                o[i * blk:(i + 1) * blk, :].astype(o_ref.dtype))


def swa_mixer(p, sinks, batch, tp, pad, col_q, col_k, col_v):
    nb = tp // ROW_BLOCK
    qw = ATTN_HEADS * HEAD_DIM
    kw = ATTN_KV_HEADS * HEAD_DIM
    assert col_q % qw == 0 and col_k % kw == 0 and col_v % kw == 0 and pad + N_META == ROW_BLOCK
    qc, kc, vc = col_q // qw, col_k // kw, col_v // kw
    cur = lambda c: (lambda b, j, s: (b * nb + j, c))
    prev = lambda c: (lambda b, j, s: (b * nb + jnp.maximum(j - 1, 0), c))
    meta = lambda c: (lambda b, j, s: (b * nb, c))
    grid_spec = pltpu.PrefetchScalarGridSpec(
        num_scalar_prefetch=1,
        grid=(batch, nb),
        in_specs=[pl.BlockSpec((ROW_BLOCK, qw), cur(qc)),
                  pl.BlockSpec((ROW_BLOCK, kw), prev(kc)),
                  pl.BlockSpec((ROW_BLOCK, kw), cur(kc)),
                  pl.BlockSpec((ROW_BLOCK, kw), meta(kc)),
                  pl.BlockSpec((ROW_BLOCK, kw), prev(vc)),
                  pl.BlockSpec((ROW_BLOCK, kw), cur(vc)),
                  pl.BlockSpec((ROW_BLOCK, kw), meta(vc))],
        out_specs=pl.BlockSpec((ROW_BLOCK, qw), lambda b, j, s: (b * nb + j, 0)),
    )
    return pl.pallas_call(
        functools.partial(_swa_kernel, slopes=_alibi_slopes(ATTN_HEADS), pad=pad),
        grid_spec=grid_spec,
        out_shape=jax.ShapeDtypeStruct((batch * tp, qw), BF16),
        compiler_params=_cparams(("parallel", "arbitrary")),
    )(sinks.astype(F32), p, p, p, p, p, p, p)


def _split3(x):
    p1 = x.astype(BF16)
    r1 = x - p1.astype(F32)
    p2 = r1.astype(BF16)
    p3 = (r1 - p2.astype(F32)).astype(BF16)
    return p1, p2, p3


def _rwkv_kernel(pr_ref, pk_ref, pv_ref, pl_ref, par_ref, mul_ref, w2_ref, a2_ref, g2_ref,
                 seg_ref, o_ref, s_ref, y_ref, cr_ref, ck_ref, cv_ref, cl_ref, *, pad):
    tb, width = pr_ref.shape
    npair = width // LANES
    c = CHUNK
    nch = tb // c
    first = pl.program_id(1) == 0

    @pl.when(first)
    def _():
        s_ref[...] = jnp.zeros_like(s_ref)
        cr_ref[...] = jnp.zeros_like(cr_ref)
        ck_ref[...] = jnp.zeros_like(ck_ref)
        cv_ref[...] = jnp.zeros_like(cv_ref)
        cl_ref[...] = jnp.zeros_like(cl_ref)

    row1 = lax.broadcasted_iota(jnp.int32, (tb, 1), 0)
    valid = (row1 >= pad) | jnp.logical_not(first)

    def shift_mix(x_ref, c_ref, mu):
        x = jnp.where(valid, x_ref[...], 0.0)
        prev = jnp.where(row1 == 0, c_ref[0:1, :], pltpu.roll(x, 1, axis=0))
        c_ref[0:1, :] = x[tb - 1:tb, :]
        return x + (prev - x) * mu

    par = par_ref[...]
    mu_r, mu_k, mu_v = par[0:1], par[1:2], par[2:3]
    w0, a0, k_k, k_a, r_k = par[3:4], par[4:5], par[5:6], par[6:7], par[7:8]
    ln_w, ln_b = par[8:9], par[9:10]

    r = shift_mix(pr_ref, cr_ref, mu_r)
    k = shift_mix(pk_ref, ck_ref, mu_k)
    v = shift_mix(pv_ref, cv_ref, mu_v)
    lo = shift_mix(pl_ref, cl_ref, mul_ref[...])
    wl = lo[:, 0:DECAY_RANK]
    al = lo[:, DECAY_RANK:DECAY_RANK + ICLR_RANK]
    gl = lo[:, DECAY_RANK + ICLR_RANK:DECAY_RANK + ICLR_RANK + g2_ref.shape[0]]

    seg = seg_ref[...]

    def seg_sum(x):
        xs = jnp.concatenate([x[:, i * LANES:(i + 1) * LANES] for i in range(npair)], axis=0)
        hi = xs.astype(BF16)
        lo_ = (xs - hi.astype(F32)).astype(BF16)
        ss = _dot(hi, seg) + _dot(lo_, seg)
        return jnp.concatenate([ss[i * tb:(i + 1) * tb, :] for i in range(npair)], axis=1)

    z = -(w0 + _dot(jnp.tanh(wl).astype(BF16), w2_ref[...]))
    w = -(jnp.maximum(z, 0.0) + jnp.log(1.0 + jnp.exp(-jnp.abs(z)))) - 0.5
    lw = -jnp.exp(w)
    a = _sigmoid(a0 + _dot(al.astype(BF16), a2_ref[...]))
    gate = _dot(_sigmoid(gl).astype(BF16), g2_ref[...])

    kk = k * k_k
    kk = kk / jnp.maximum(jnp.sqrt(seg_sum(kk * kk)), 1e-12)
    k2 = k * (1.0 + (a - 1.0) * k_a)
    bvec = kk * a
    bonus = seg_sum(r * k2 * r_k) * v

    ti = lax.broadcasted_iota(jnp.int32, (tb, tb), 0)
    tj = lax.broadcasted_iota(jnp.int32, (tb, tb), 1)
    tri = jnp.where((ti // c == tj // c) & (tj <= ti), 1.0, 0.0).astype(BF16)
    l1, l2, l3 = _split3(lw)
    cum = _dot(tri, l1) + _dot(tri, l2) + _dot(tri, l3)

    r_hat = r * jnp.exp(cum)
    a_hat = -kk * jnp.exp(cum - lw)
    dinv = jnp.exp(-cum)
    b_til = bvec * dinv
    k_til = k2 * dinv

    lane = lax.broadcasted_iota(jnp.int32, (c, LANES), 1)
    m0 = lane < RWKV_N

    def stack(x):
        return jnp.concatenate([jnp.where(m0, x, 0.0), jnp.where(m0, 0.0, x)], axis=0)

    ri = lax.broadcasted_iota(jnp.int32, (2 * c, 2 * c), 0)
    ci = lax.broadcasted_iota(jnp.int32, (2 * c, 2 * c), 1)
    same = (ri // c) == (ci // c)
    strict = same & (ci < ri)
    incl = same & (ci <= ri)
    eye = jnp.where(ri == ci, 1.0, 0.0)

    units = [(ch, pr) for ch in range(nch) for pr in range(npair)]

    def sl(u):
        ch, pr = u
        return slice(ch * c, (ch + 1) * c), slice(pr * LANES, (pr + 1) * LANES)

    a_s, r_s, v_s, bk, d_tot, sc = {}, {}, {}, {}, {}, {}
    for u in units:
        rs, ls = sl(u)
        last = rs.stop - 1
        cum_last = cum[last:last + 1, ls]
        dec_rest = jnp.exp(cum_last - cum[rs, ls])
        d_tot[u] = jnp.exp(cum_last)
        a_s[u] = stack(a_hat[rs, ls]).astype(BF16)
        r_s[u] = stack(r_hat[rs, ls]).astype(BF16)
        v_s[u] = stack(v[rs, ls]).astype(BF16)
        bk[u] = jnp.concatenate([stack(bvec[rs, ls] * dec_rest), stack(k2[rs, ls] * dec_rest)],
                                axis=0).astype(BF16)
        bt = b_til[rs, ls]
        kt = k_til[rs, ls]
        rhs4 = jnp.concatenate([bt, bt, kt, kt], axis=0).astype(BF16)
        sc[u] = _dot_nt(jnp.concatenate([a_s[u], r_s[u]], axis=0), rhs4)

    lp, tinv, m_rb, mv = {}, {}, {}, {}
    for u in units:
        l_ab = jnp.where(strict, sc[u][0:2 * c, 0:2 * c], 0.0)
        m_ak = jnp.where(strict, sc[u][0:2 * c, 2 * c:4 * c], 0.0)
        m_rk = jnp.where(incl, sc[u][2 * c:4 * c, 2 * c:4 * c], 0.0)
        m_rb[u] = jnp.where(incl, sc[u][2 * c:4 * c, 0:2 * c], 0.0).astype(BF16)
        lp[u] = l_ab.astype(BF16)
        tinv[u] = eye + l_ab
        mv[u] = _dot(jnp.concatenate([m_ak, m_rk], axis=0).astype(BF16), v_s[u])
    del sc

    nlev = int(math.log2(c)) - 1
    for lev in range(nlev):
        for u in units:
            sq = _dot(lp[u], lp[u]).astype(BF16)
            tinv[u] = tinv[u] + _dot(tinv[u].astype(BF16), sq)
            lp[u] = sq

    g_s, u_loc = {}, {}
    for u in units:
        gu = _dot(tinv[u].astype(BF16),
                  jnp.concatenate([a_s[u], mv[u][0:2 * c].astype(BF16)], axis=1))
        g_s[u] = gu[:, 0:LANES].astype(BF16)
        u_loc[u] = gu[:, LANES:2 * LANES]

    for ch in range(nch):
        xs, us = {}, {}
        for pr in range(npair):
            u = (ch, pr)
            state = s_ref[pr]
            xs[u] = _dot_nt(jnp.concatenate([g_s[u], r_s[u]], axis=0), state.astype(BF16))
        for pr in range(npair):
            u = (ch, pr)
            rs, ls = sl(u)
            us[u] = (xs[u][0:2 * c] + u_loc[u]).astype(BF16)
            ys = xs[u][2 * c:4 * c] + mv[u][2 * c:4 * c] + _dot(m_rb[u], us[u])
            y_ref[rs, ls] = ys[0:c] + ys[c:2 * c]
            uv = jnp.concatenate([us[u], v_s[u]], axis=0)
            s_ref[pr] = s_ref[pr] * d_tot[u] + _dot_tn(uv, bk[u])

    y = y_ref[...]
    inv_n = 1.0 / RWKV_N
    mean = seg_sum(y) * inv_n
    yc = y - mean
    var = seg_sum(yc * yc) * inv_n
    yn = yc * lax.rsqrt(var + GN_EPS) * ln_w + ln_b
    o_ref[...] = ((yn + bonus) * gate).astype(o_ref.dtype)


def rwkv_mixer(p, par, mu_lora, w2, a2, g2, batch, tp, pad, col0, col_lora, width):
    nb = tp // ROW_BLOCK
    lw = mu_lora.shape[1]
    assert col0 % width == 0 and col_lora % lw == 0 and pad < ROW_BLOCK
    c_r = col0 // width
    c_l = col_lora // lw
    seg = (np.arange(LANES)[:, None] // RWKV_N == np.arange(LANES)[None, :] // RWKV_N)
    seg = jnp.asarray(seg, BF16)
    rows = lambda b, t: b * nb + t
    const = lambda b, t: (0, 0)
    return pl.pallas_call(
        functools.partial(_rwkv_kernel, pad=pad),
        grid=(batch, nb),
        in_specs=[pl.BlockSpec((ROW_BLOCK, width), lambda b, t: (rows(b, t), c_r)),
                  pl.BlockSpec((ROW_BLOCK, width), lambda b, t: (rows(b, t), c_r + 1)),
                  pl.BlockSpec((ROW_BLOCK, width), lambda b, t: (rows(b, t), c_r + 2)),
                  pl.BlockSpec((ROW_BLOCK, lw), lambda b, t: (rows(b, t), c_l)),
                  pl.BlockSpec((16, width), const),
                  pl.BlockSpec((1, lw), const),
                  pl.BlockSpec((DECAY_RANK, width), const),
                  pl.BlockSpec((ICLR_RANK, width), const),
                  pl.BlockSpec((g2.shape[0], width), const),
                  pl.BlockSpec((LANES, LANES), const)],
        out_specs=pl.BlockSpec((ROW_BLOCK, width), lambda b, t: (rows(b, t), 0)),
        out_shape=jax.ShapeDtypeStruct((batch * tp, width), BF16),
        scratch_shapes=[pltpu.VMEM((width // LANES, LANES, LANES), F32),
                        pltpu.VMEM((ROW_BLOCK, width), F32),
                        pltpu.VMEM((8, width), F32), pltpu.VMEM((8, width), F32),
                        pltpu.VMEM((8, width), F32), pltpu.VMEM((8, lw), F32)],
        compiler_params=_cparams(("parallel", "arbitrary")),
    )(p, p, p, p, par, mu_lora, w2, a2, g2, seg)


def _pad_cols(w, n):
    return jnp.pad(w, ((0, 0), (0, n - w.shape[1])))


def _pad_rows(w, n):
    return jnp.pad(w, ((0, n - w.shape[0]), (0, 0)))


def kernel(x, meta_tokens, norm_mix, w_in, conv_w, attn_sinks, rwkv_mu, rwkv_w0, rwkv_w2, rwkv_a0, rwkv_a2, rwkv_g2, rwkv_k_k, rwkv_k_a, rwkv_r_k, rwkv_ln_w, rwkv_ln_b, w_out, norm_ffn, w_ffn_in, w_ffn_out, norm_final):
    batch, seq, d = x.shape
    depth = w_in.shape[0]
    conv_width = conv_w.shape[1]
    attn_width = ATTN_HEADS * HEAD_DIM
    attn_cols = attn_width + 2 * ATTN_KV_HEADS * HEAD_DIM
    rwkv_width = rwkv_w0.shape[1]
    ffn_hidden = w_ffn_out.shape[1]
    assert seq % ROW_BLOCK == 0
    pad = ROW_BLOCK - N_META
    tp = seq + ROW_BLOCK
    nb = tp // ROW_BLOCK
    m = batch * tp

    lora_w = 1024
    gate_pad = 512
    kv_width = ATTN_KV_HEADS * HEAD_DIM
    c_conv, c_attn = CONV_K * conv_width, attn_cols
    col_rwkv = 0
    col_q = 3 * rwkv_width
    col_conv = col_q + attn_width
    col_k = col_conv + c_conv
    col_v = col_k + kv_width
    col_lora = col_v + kv_width
    in_cols = col_lora + lora_w
    hid_pad = -(-ffn_hidden // 1024) * 1024

    tm = m // 13 if m % 13 == 0 else ROW_BLOCK
    tm_half = tm // 2 if tm % 16 == 0 else tm
    tt_conv = tp // 13 if tp % (13 * 8) == 0 else ROW_BLOCK

    h = None
    for l in range(depth):
        wi = w_in[l]
        o_attn, o_rwkv = c_conv, c_conv + c_attn
        w_in_l = _pad_cols(jnp.concatenate([
            wi[:, o_rwkv:o_rwkv + 3 * rwkv_width],
            wi[:, o_attn:o_attn + attn_width],
            wi[:, 0:c_conv],
            wi[:, o_attn + attn_width:o_rwkv],
            wi[:, o_rwkv + 3 * rwkv_width:]], axis=1), in_cols).astype(BF16)
        mu = rwkv_mu[l]
        par = jnp.zeros((16, rwkv_width), F32)
        par = par.at[0].set(mu[0:rwkv_width])
        par = par.at[1].set(mu[rwkv_width:2 * rwkv_width])
        par = par.at[2].set(mu[2 * rwkv_width:3 * rwkv_width])
        par = par.at[3].set(rwkv_w0[l]).at[4].set(rwkv_a0[l]).at[5].set(rwkv_k_k[l])
        par = par.at[6].set(rwkv_k_a[l]).at[7].set(rwkv_r_k[l])
        par = par.at[8].set(rwkv_ln_w[l]).at[9].set(rwkv_ln_b[l])
        mu_lora = _pad_cols(mu[3 * rwkv_width:].reshape(1, -1), lora_w)
        g2 = _pad_rows(rwkv_g2[l], gate_pad).astype(BF16)

        if l == 0:
            h, u = embed_norm(x.reshape(batch * seq, d), meta_tokens.astype(x.dtype), norm_mix[l],
                              batch, nb)
        else:
            u = rmsnorm(h, norm_mix[l], BF16)
        p = matmul(u, w_in_l, F32, tm, 512)
        y_conv = conv_mixer(p, conv_w[l], batch, tp, pad, col_conv, conv_width, tt_conv)
        y_attn = swa_mixer(p, attn_sinks[l], batch, tp, pad, col_q, col_k, col_v)
        y_rwkv = rwkv_mixer(p, par, mu_lora, rwkv_w2[l].astype(BF16), rwkv_a2[l].astype(BF16), g2,
                            batch, tp, pad, col_rwkv, col_lora, rwkv_width)
        wo = w_out[l].astype(BF16)
        h = mix_out_matmul(y_conv, y_attn, y_rwkv,
                           wo[0:conv_width], wo[conv_width:conv_width + attn_width],
                           wo[conv_width + attn_width:], h, tm, 512)

        u = rmsnorm(h, norm_ffn[l], BF16)
        wg = _pad_cols(w_ffn_in[l][:, 0:ffn_hidden], hid_pad).astype(BF16)
        wu = _pad_cols(w_ffn_in[l][:, ffn_hidden:], hid_pad).astype(BF16)
        act = swiglu_matmul(u, wg, wu, tm, 512)
        wd = _pad_rows(w_ffn_out[l], hid_pad).astype(BF16)
        h = matmul_residual(act, wd, h, tm_half, 256)

    out = final_norm(h, norm_final, batch, nb, x.dtype)
    return out.reshape(batch, seq, d)
```

```python
import functools
import math

import numpy as np
import jax
import jax.numpy as jnp
from jax import lax
from jax.experimental import pallas as pl
from jax.experimental.pallas import tpu as pltpu

F32 = jnp.float32
BF16 = jnp.bfloat16

N_META = 16
NORM_EPS = 1e-5
GN_EPS = 64e-5
CONV_K = 3
HEAD_DIM = 128
ATTN_HEADS = 12
ATTN_KV_HEADS = 4
ATTN_GROUP = ATTN_HEADS // ATTN_KV_HEADS
WINDOW = 128
RWKV_N = 64
DECAY_RANK = 128
ICLR_RANK = 128
GATE_RANK = 480

LANES = 128
ROW_BLOCK = 128
CHUNK = 64
RWKV_COLS = 512
VMEM_LIMIT = 56 * 1024 * 1024
NEG = -1e30


def _alibi_slopes(n):
    def pow2_slopes(m):
        start = 2.0 ** (-8.0 / m)
        return [start ** (i + 1) for i in range(m)]
    c = 2 ** int(math.floor(math.log2(n)))
    s = pow2_slopes(c)
    if c < n:
        s = s + pow2_slopes(2 * c)[0::2][: n - c]
    return [float(np.float32(v)) for v in s]


def _cparams(sem):
    return pltpu.CompilerParams(dimension_semantics=sem, vmem_limit_bytes=VMEM_LIMIT)


def _dot(a, b):
    return jnp.dot(a, b, preferred_element_type=F32)


def _dot_nt(a, b):
    return lax.dot_general(a, b, (((1,), (1,)), ((), ())), preferred_element_type=F32)


def _dot_tn(a, b):
    return lax.dot_general(a, b, (((0,), (0,)), ((), ())), preferred_element_type=F32)


def _sigmoid(x):
    return 1.0 / (1.0 + jnp.exp(-x))


def _lane_partial_sumsq(x):
    sq = x * x
    acc = sq[:, 0:LANES]
    for c in range(1, x.shape[1] // LANES):
        acc = acc + sq[:, c * LANES:(c + 1) * LANES]
    return acc


def _row_scale(ssq_ref, d):
    return lax.rsqrt(jnp.sum(ssq_ref[...], axis=-1, keepdims=True) * (1.0 / d) + NORM_EPS)


def _embed_kernel(x_ref, meta_ref, h_ref, hb_ref, ssq_ref):
    pad = h_ref.shape[0] - meta_ref.shape[0]

    @pl.when(pl.program_id(1) == 0)
    def _():
        h_ref[0:pad, :] = jnp.zeros((pad, h_ref.shape[1]), h_ref.dtype)
        h_ref[pad:, :] = meta_ref[...]

    @pl.when(pl.program_id(1) > 0)
    def _():
        h_ref[...] = x_ref[...]

    h = h_ref[...]
    hb_ref[...] = h.astype(hb_ref.dtype)
    ssq_ref[...] = _lane_partial_sumsq(h)


def embed(x2, meta_tokens, batch, nb):
    d = x2.shape[1]
    nx = nb - 1
    m = batch * nb * ROW_BLOCK
    blk = pl.BlockSpec((ROW_BLOCK, d), lambda b, i: (b * nb + i, 0))
    return pl.pallas_call(
        _embed_kernel,
        grid=(batch, nb),
        in_specs=[pl.BlockSpec((ROW_BLOCK, d), lambda b, i: (b * nx + jnp.maximum(i - 1, 0), 0)),
                  pl.BlockSpec((N_META, d), lambda b, i: (0, 0))],
        out_specs=[blk, blk, pl.BlockSpec((ROW_BLOCK, LANES), lambda b, i: (b * nb + i, 0))],
        out_shape=[jax.ShapeDtypeStruct((m, d), F32),
                   jax.ShapeDtypeStruct((m, d), BF16),
                   jax.ShapeDtypeStruct((m, LANES), F32)],
        compiler_params=_cparams(("parallel", "arbitrary")),
    )(x2, meta_tokens)


def _final_norm_kernel(h_ref, g_ref, o_ref):
    x = h_ref[...]
    ms = jnp.mean(x * x, axis=-1, keepdims=True)
    o_ref[...] = (x * lax.rsqrt(ms + NORM_EPS) * g_ref[...]).astype(o_ref.dtype)


def final_norm(h, g, batch, nb, out_dtype):
    d = h.shape[1]
    nx = nb - 1
    return pl.pallas_call(
        _final_norm_kernel,
        grid=(batch, nx),
        in_specs=[pl.BlockSpec((ROW_BLOCK, d), lambda b, i: (b * nb + i + 1, 0)),
                  pl.BlockSpec((1, d), lambda b, i: (0, 0))],
        out_specs=pl.BlockSpec((ROW_BLOCK, d), lambda b, i: (b * nx + i, 0)),
        out_shape=jax.ShapeDtypeStruct((batch * nx * ROW_BLOCK, d), out_dtype),
        compiler_params=_cparams(("parallel", "parallel")),
    )(h, g.reshape(1, d))


def _normed_mm_kernel(a_ref, ssq_ref, w_ref, o_ref):
    s = _row_scale(ssq_ref, a_ref.shape[1])
    o_ref[...] = (_dot(a_ref[...], w_ref[...]) * s).astype(o_ref.dtype)


def normed_matmul(hb, ssq, w, layer, n, out_dtype, tm, tn):
    m, k = hb.shape
    tn = min(tn, n)
    w_spec = (pl.BlockSpec((k, tn), lambda i, j: (0, j)) if layer is None else
              pl.BlockSpec((None, k, tn), lambda i, j: (layer, 0, j)))
    return pl.pallas_call(
        _normed_mm_kernel,
        grid=(m // tm, n // tn),
        in_specs=[pl.BlockSpec((tm, k), lambda i, j: (i, 0)),
                  pl.BlockSpec((tm, LANES), lambda i, j: (i, 0)),
                  w_spec],
        out_specs=pl.BlockSpec((tm, tn), lambda i, j: (i, j)),
        out_shape=jax.ShapeDtypeStruct((m, n), out_dtype),
        compiler_params=_cparams(("parallel", "arbitrary")),
    )(hb, ssq, w)


def _normed_swiglu_kernel(a_ref, ssq_ref, wg_ref, wu_ref, o_ref):
    s = _row_scale(ssq_ref, a_ref.shape[1])
    a = a_ref[...]
    g = _dot(a, wg_ref[...]) * s
    u = _dot(a, wu_ref[...]) * s
    o_ref[...] = (g * _sigmoid(g) * u).astype(o_ref.dtype)


def normed_swiglu(hb, ssq, w, layer, hidden, tm, tn):
    m, k = hb.shape
    nt = hidden // tn
    assert hidden % tn == 0
    return pl.pallas_call(
        _normed_swiglu_kernel,
        grid=(m // tm, nt),
        in_specs=[pl.BlockSpec((tm, k), lambda i, j: (i, 0)),
                  pl.BlockSpec((tm, LANES), lambda i, j: (i, 0)),
                  pl.BlockSpec((None, k, tn), lambda i, j: (layer, 0, j)),
                  pl.BlockSpec((None, k, tn), lambda i, j: (layer, 0, j + nt))],
        out_specs=pl.BlockSpec((tm, tn), lambda i, j: (i, j)),
        out_shape=jax.ShapeDtypeStruct((m, hidden), BF16),
        compiler_params=_cparams(("parallel", "arbitrary")),
    )(hb, ssq, w, w)


def _emit_residual(acc, r_ref, h_ref, hb_ref, ssq_ref):
    h = r_ref[...] + acc
    h_ref[...] = h
    hb_ref[...] = h.astype(hb_ref.dtype)
    part = _lane_partial_sumsq(h)

    @pl.when(pl.program_id(1) == 0)
    def _():
        ssq_ref[...] = part

    @pl.when(pl.program_id(1) > 0)
    def _():
        ssq_ref[...] += part


def _mix_out_kernel(a1_ref, a2_ref, a3_ref, w_ref, r_ref, h_ref, hb_ref, ssq_ref):
    k1, k2 = a1_ref.shape[1], a2_ref.shape[1]
    acc = _dot(a1_ref[...], w_ref[0:k1, :])
    acc = acc + _dot(a2_ref[...], w_ref[k1:k1 + k2, :])
    acc = acc + _dot(a3_ref[...], w_ref[k1 + k2:, :])
    _emit_residual(acc, r_ref, h_ref, hb_ref, ssq_ref)


def _mm_res_kernel(a_ref, w_ref, r_ref, h_ref, hb_ref, ssq_ref):
    _emit_residual(_dot(a_ref[...], w_ref[...]), r_ref, h_ref, hb_ref, ssq_ref)


def _residual_call(body, acts, w, layer, res, tm, tn):
    m, n = res.shape
    k = w.shape[1]
    tn = min(tn, n)
    tile = pl.BlockSpec((tm, tn), lambda i, j: (i, j))
    return pl.pallas_call(
        body,
        grid=(m // tm, n // tn),
        in_specs=[pl.BlockSpec((tm, a.shape[1]), lambda i, j: (i, 0)) for a in acts] + [
            pl.BlockSpec((None, k, tn), lambda i, j: (layer, 0, j)), tile],
        out_specs=[tile, tile, pl.BlockSpec((tm, LANES), lambda i, j: (i, 0))],
        out_shape=[jax.ShapeDtypeStruct((m, n), F32),
                   jax.ShapeDtypeStruct((m, n), BF16),
                   jax.ShapeDtypeStruct((m, LANES), F32)],
        compiler_params=_cparams(("parallel", "arbitrary")),
    )(*acts, w, res)


def _conv_kernel(cb_ref, cc_ref, ch_ref, w_ref, o_ref, carry_ref, *, pad):
    first = pl.program_id(1) == 0

    @pl.when(first)
    def _():
        carry_ref[...] = jnp.zeros_like(carry_ref)

    u = cc_ref[...].astype(F32) * ch_ref[...].astype(F32)
    tt = u.shape[0]
    row = lax.broadcasted_iota(jnp.int32, u.shape, 0)
    u = jnp.where((row >= pad) | jnp.logical_not(first), u, 0.0)
    c0 = carry_ref[0:1, :]
    c1 = carry_ref[1:2, :]
    u1 = jnp.where(row == 0, c1, pltpu.roll(u, 1, axis=0))
    u2 = jnp.where(row == 0, c0, jnp.where(row == 1, c1, pltpu.roll(u, 2, axis=0)))
    carry_ref[0:2, :] = u[tt - 2:tt, :]
    y = u2 * w_ref[0:1, :]
    y = y + u1 * w_ref[1:2, :]
    y = y + u * w_ref[2:3, :]
    o_ref[...] = (cb_ref[...].astype(F32) * y).astype(o_ref.dtype)


def conv_mixer(p, conv_w, batch, tp, pad, col0, width, tt):
    nt = tp // tt
    assert pad <= tt and col0 % width == 0
    cb = col0 // width
    w = jnp.zeros((8, width), F32).at[0:CONV_K].set(conv_w.T)
    return pl.pallas_call(
        functools.partial(_conv_kernel, pad=pad),
        grid=(batch, nt),
        in_specs=[pl.BlockSpec((tt, width), lambda b, t: (b * nt + t, cb)),
                  pl.BlockSpec((tt, width), lambda b, t: (b * nt + t, cb + 1)),
                  pl.BlockSpec((tt, width), lambda b, t: (b * nt + t, cb + 2)),
                  pl.BlockSpec((8, width), lambda b, t: (0, 0))],
        out_specs=pl.BlockSpec((tt, width), lambda b, t: (b * nt + t, 0)),
        out_shape=jax.ShapeDtypeStruct((batch * tp, width), BF16),
        scratch_shapes=[pltpu.VMEM((8, width), F32)],
        compiler_params=_cparams(("parallel", "arbitrary")),
    )(p, p, p, w)


def _swa_kernel(sink_ref, q_ref, kp_ref, kc_ref, km_ref, vp_ref, vc_ref, vm_ref, o_ref, *,
                slopes, pad):
    j = pl.program_id(1)
    blk = ROW_BLOCK
    g = ATTN_GROUP
    scale = HEAD_DIM ** -0.5
    rows = g * blk
    ri = lax.broadcasted_iota(jnp.int32, (rows, 3 * blk), 0)
    ci = lax.broadcasted_iota(jnp.int32, (rows, 3 * blk), 1)
    q_pos = j * blk + ri % blk - pad
    k_pos = jnp.where(ci < blk, ci, (j - 2) * blk + ci) - pad
    dist = q_pos - k_pos
    is_meta = ci < blk
    ok = (is_meta & (k_pos >= 0) & (k_pos <= q_pos)) | (
        (ci >= blk) & (dist >= 0) & (dist < WINDOW) & (k_pos >= N_META))
    distf = jnp.where(is_meta, 0, dist).astype(F32)
    rgrp = lax.broadcasted_iota(jnp.int32, (rows, 1), 0) // blk
    for h in range(ATTN_KV_HEADS):
        sl = slice(h * HEAD_DIM, (h + 1) * HEAD_DIM)
        kb = jnp.concatenate([km_ref[:, sl], kp_ref[:, sl], kc_ref[:, sl]], axis=0).astype(BF16)
        vb = jnp.concatenate([vm_ref[:, sl], vp_ref[:, sl], vc_ref[:, sl]], axis=0).astype(BF16)
        q3 = jnp.concatenate(
            [q_ref[:, (h * g + i) * HEAD_DIM:(h * g + i + 1) * HEAD_DIM] for i in range(g)],
            axis=0).astype(BF16)
        slope = jnp.full((rows, 1), slopes[h * g], F32)
        sink = jnp.full((rows, 1), sink_ref[h * g], F32)
        for i in range(1, g):
            slope = jnp.where(rgrp == i, slopes[h * g + i], slope)
            sink = jnp.where(rgrp == i, sink_ref[h * g + i], sink)
        s = _dot_nt(q3, kb) * scale - slope * distf
        s = jnp.where(ok, s, NEG)
        mx = jnp.maximum(jnp.max(s, axis=-1, keepdims=True), sink)
        e = jnp.exp(s - mx)
        den = jnp.sum(e, axis=-1, keepdims=True) + jnp.exp(sink - mx)
        o = _dot(e.astype(BF16), vb) / den
        for i in range(g):
            o_ref[:, (h * g + i) * HEAD_DIM:(h * g + i + 1) * HEAD_DIM] = (
                o[i * blk:(i + 1) * blk, :].astype(o_ref.dtype))


def swa_mixer(p, sinks, batch, tp, pad, col_q, col_k, col_v):
    nb = tp // ROW_BLOCK
    qw = ATTN_HEADS * HEAD_DIM
    kw = ATTN_KV_HEADS * HEAD_DIM
    assert col_q % qw == 0 and col_k % kw == 0 and col_v % kw == 0 and pad + N_META == ROW_BLOCK
    qc, kc, vc = col_q // qw, col_k // kw, col_v // kw
    cur = lambda c: (lambda b, j, s: (b * nb + j, c))
    prev = lambda c: (lambda b, j, s: (b * nb + jnp.maximum(j - 1, 0), c))
    meta = lambda c: (lambda b, j, s: (b * nb, c))
    grid_spec = pltpu.PrefetchScalarGridSpec(
        num_scalar_prefetch=1,
        grid=(batch, nb),
        in_specs=[pl.BlockSpec((ROW_BLOCK, qw), cur(qc)),
                  pl.BlockSpec((ROW_BLOCK, kw), prev(kc)),
                  pl.BlockSpec((ROW_BLOCK, kw), cur(kc)),
                  pl.BlockSpec((ROW_BLOCK, kw), meta(kc)),
                  pl.BlockSpec((ROW_BLOCK, kw), prev(vc)),
                  pl.BlockSpec((ROW_BLOCK, kw), cur(vc)),
                  pl.BlockSpec((ROW_BLOCK, kw), meta(vc))],
        out_specs=pl.BlockSpec((ROW_BLOCK, qw), lambda b, j, s: (b * nb + j, 0)),
    )
    return pl.pallas_call(
        functools.partial(_swa_kernel, slopes=_alibi_slopes(ATTN_HEADS), pad=pad),
        grid_spec=grid_spec,
        out_shape=jax.ShapeDtypeStruct((batch * tp, qw), BF16),
        compiler_params=_cparams(("parallel", "arbitrary")),
    )(sinks.astype(F32), p, p, p, p, p, p, p)


def _split3(x):
    p1 = x.astype(BF16)
    r1 = x - p1.astype(F32)
    p2 = r1.astype(BF16)
    p3 = (r1 - p2.astype(F32)).astype(BF16)
    return p1, p2, p3


def _rwkv_kernel(*refs, pad, nblk):
    pr_refs = refs[0:nblk]
    pk_refs = refs[nblk:2 * nblk]
    pv_refs = refs[2 * nblk:3 * nblk]
    (pl_ref, par_ref, mul_ref, w2_ref, a2_ref, g2_ref, seg_ref, o_ref,
     s_ref, y_ref, cr_ref, ck_ref, cv_ref, cl_ref) = refs[3 * nblk:]
    tb, width = o_ref.shape
    npair = width // LANES
    c = CHUNK
    nch = tb // c
    first = pl.program_id(1) == 0

    @pl.when(first)
    def _():
        s_ref[...] = jnp.zeros_like(s_ref)
        cr_ref[...] = jnp.zeros_like(cr_ref)
        ck_ref[...] = jnp.zeros_like(ck_ref)
        cv_ref[...] = jnp.zeros_like(cv_ref)
        cl_ref[...] = jnp.zeros_like(cl_ref)

    row1 = lax.broadcasted_iota(jnp.int32, (tb, 1), 0)
    valid = (row1 >= pad) | jnp.logical_not(first)

    def shift_mix(x, c_ref, mu):
        x = jnp.where(valid, x.astype(F32), 0.0)
        prev = jnp.where(row1 == 0, c_ref[0:1, :], pltpu.roll(x, 1, axis=0))
        c_ref[0:1, :] = x[tb - 1:tb, :]
        return x + (prev - x) * mu

    def wide(block_refs):
        return jnp.concatenate([b[...] for b in block_refs], axis=1)

    par = par_ref[...]
    mu_r, mu_k, mu_v = par[0:1], par[1:2], par[2:3]
    w0, a0, k_k, k_a, r_k = par[3:4], par[4:5], par[5:6], par[6:7], par[7:8]
    ln_w, ln_b = par[8:9], par[9:10]

    r = shift_mix(wide(pr_refs), cr_ref, mu_r)
    k = shift_mix(wide(pk_refs), ck_ref, mu_k)
    v = shift_mix(wide(pv_refs), cv_ref, mu_v)
    lo = shift_mix(pl_ref[...], cl_ref, mul_ref[...])
    wl = lo[:, 0:DECAY_RANK]
    al = lo[:, DECAY_RANK:DECAY_RANK + ICLR_RANK]
    gl = lo[:, DECAY_RANK + ICLR_RANK:DECAY_RANK + ICLR_RANK + g2_ref.shape[0]]

    seg = seg_ref[...]

    def seg_sum(x):
        xs = jnp.concatenate([x[:, i * LANES:(i + 1) * LANES] for i in range(npair)], axis=0)
        hi = xs.astype(BF16)
        lo_ = (xs - hi.astype(F32)).astype(BF16)
        ss = _dot(jnp.concatenate([hi, lo_], axis=1), seg)
        return jnp.concatenate([ss[i * tb:(i + 1) * tb, :] for i in range(npair)], axis=1)

    z = -(w0 + _dot(jnp.tanh(wl).astype(BF16), w2_ref[...]))
    w = -(jnp.maximum(z, 0.0) + jnp.log(1.0 + jnp.exp(-jnp.abs(z)))) - 0.5
    lw = -jnp.exp(w)
    a = _sigmoid(a0 + _dot(al.astype(BF16), a2_ref[...]))
    gate = _dot(_sigmoid(gl).astype(BF16), g2_ref[...])

    kk = k * k_k
    kk = kk / jnp.maximum(jnp.sqrt(seg_sum(kk * kk)), 1e-12)
    k2 = k * (1.0 + (a - 1.0) * k_a)
    bvec = kk * a
    bonus = seg_sum(r * k2 * r_k) * v

    ti = lax.broadcasted_iota(jnp.int32, (tb, tb), 0)
    tj = lax.broadcasted_iota(jnp.int32, (tb, tb), 1)
    tri = jnp.where((ti // c == tj // c) & (tj <= ti), 1.0, 0.0).astype(BF16)
    cum = _dot(jnp.concatenate([tri, tri, tri], axis=1), jnp.concatenate(_split3(lw), axis=0))

    r_hat = r * jnp.exp(cum)
    a_hat = -kk * jnp.exp(cum - lw)
    dinv = jnp.exp(-cum)
    b_til = bvec * dinv
    k_til = k2 * dinv

    ri = lax.broadcasted_iota(jnp.int32, (2 * c, 2 * c), 0)
    ci = lax.broadcasted_iota(jnp.int32, (2 * c, 2 * c), 1)
    head_mask = jnp.where((ri // c) == (ci // c), 1.0, 0.0).astype(BF16)

    def dup(xb):
        return jnp.concatenate([xb, xb], axis=0) * head_mask

    rt = lax.broadcasted_iota(jnp.int32, (c, 2 * c), 0)
    cs = lax.broadcasted_iota(jnp.int32, (c, 2 * c), 1) % c
    strict = cs < rt
    incl = cs <= rt
    eye = jnp.where(cs == rt, 1.0, 0.0)

    units = [(ch, pr) for ch in range(nch) for pr in range(npair)]

    def sl(u):
        ch, pr = u
        return slice(ch * c, (ch + 1) * c), slice(pr * LANES, (pr + 1) * LANES)

    a_s, gr, v_s, bk, d_tot, sc = {}, {}, {}, {}, {}, {}
    for u in units:
        rs, ls = sl(u)
        last = rs.stop - 1
        cum_last = cum[last:last + 1, ls]
        dec_rest = jnp.exp(cum_last - cum[rs, ls])
        d_tot[u] = jnp.exp(cum_last)
        a_u = a_hat[rs, ls].astype(BF16)
        gr[u] = r_hat[rs, ls].astype(BF16)
        a_s[u] = dup(a_u)
        v_s[u] = dup(v[rs, ls].astype(BF16))
        bk[u] = jnp.concatenate([dup((bvec[rs, ls] * dec_rest).astype(BF16)),
                                 dup((k2[rs, ls] * dec_rest).astype(BF16))], axis=0)
        rhs = jnp.concatenate([dup(b_til[rs, ls].astype(BF16)),
                               dup(k_til[rs, ls].astype(BF16))], axis=0)
        sc[u] = _dot_nt(jnp.concatenate([a_u, gr[u]], axis=0), rhs)

    lp, tinv, m_r, mak_v = {}, {}, {}, {}
    for u in units:
        l_ab = jnp.where(strict, sc[u][0:c, 0:2 * c], 0.0)
        m_ak = jnp.where(strict, sc[u][0:c, 2 * c:4 * c], 0.0).astype(BF16)
        m_r[u] = jnp.concatenate([jnp.where(incl, sc[u][c:2 * c, 0:2 * c], 0.0),
                                  jnp.where(incl, sc[u][c:2 * c, 2 * c:4 * c], 0.0)],
                                 axis=1).astype(BF16)
        lp[u] = l_ab.astype(BF16)
        tinv[u] = eye + l_ab
        mak_v[u] = _dot(m_ak, v_s[u]).astype(BF16)
    del sc

    nlev = int(math.log2(c))
    for u in units:
        lp[u] = _dot(lp[u], dup(lp[u])).astype(BF16)
    for lev in range(1, nlev - 1):
        for u in units:
            res = _dot(lp[u], jnp.concatenate([dup(lp[u]), dup(tinv[u].astype(BF16))], axis=1))
            tinv[u] = tinv[u] + res[:, 2 * c:4 * c]
            lp[u] = res[:, 0:2 * c].astype(BF16)
    for u in units:
        tinv[u] = tinv[u] + _dot(lp[u], dup(tinv[u].astype(BF16)))

    u_loc = {}
    for u in units:
        gu = _dot(tinv[u].astype(BF16), jnp.concatenate([a_s[u], dup(mak_v[u])], axis=1))
        gr[u] = jnp.concatenate([gu[:, 0:LANES].astype(BF16), gr[u]], axis=0)
        u_loc[u] = gu[:, LANES:2 * LANES]

    for ch in range(nch):
        xs = {}
        for pr in range(npair):
            u = (ch, pr)
            state = s_ref[pr]
            xs[u] = _dot_nt(gr[u], state.astype(BF16))
        for pr in range(npair):
            u = (ch, pr)
            rs, ls = sl(u)
            us = (xs[u][0:c] + u_loc[u]).astype(BF16)
            uv = jnp.concatenate([dup(us), v_s[u]], axis=0)
            y_ref[rs, ls] = xs[u][c:2 * c] + _dot(m_r[u], uv)
            s_ref[pr] = s_ref[pr] * d_tot[u] + _dot_tn(uv, bk[u])

    y = y_ref[...]
    inv_n = 1.0 / RWKV_N
    mean = seg_sum(y) * inv_n
    yc = y - mean
    var = seg_sum(yc * yc) * inv_n
    yn = yc * lax.rsqrt(var + GN_EPS) * ln_w + ln_b
    o_ref[...] = ((yn + bonus) * gate).astype(o_ref.dtype)


def rwkv_mixer(p, p_lora, par, mu_lora, w2, a2, g2, batch, tp, pad, col0, width):
    nb = tp // ROW_BLOCK
    lw = mu_lora.shape[1]
    cw = RWKV_COLS
    assert col0 % cw == 0 and width % cw == 0 and pad < ROW_BLOCK and p_lora.shape[1] == lw
    nblk = width // cw
    c0 = col0 // cw
    seg = (np.arange(LANES)[:, None] // RWKV_N == np.arange(LANES)[None, :] // RWKV_N)
    seg = jnp.asarray(np.concatenate([seg, seg], axis=0), BF16)
    const = lambda b, t: (0, 0)

    def col_block(cidx):
        return pl.BlockSpec((ROW_BLOCK, cw), lambda b, t: (b * nb + t, cidx))

    return pl.pallas_call(
        functools.partial(_rwkv_kernel, pad=pad, nblk=nblk),
        grid=(batch, nb),
        in_specs=[col_block(c0 + i) for i in range(3 * nblk)] + [
            pl.BlockSpec((ROW_BLOCK, lw), lambda b, t: (b * nb + t, 0)),
            pl.BlockSpec((16, width), const),
            pl.BlockSpec((1, lw), const),
            pl.BlockSpec((DECAY_RANK, width), const),
            pl.BlockSpec((ICLR_RANK, width), const),
            pl.BlockSpec((g2.shape[0], width), const),
            pl.BlockSpec((2 * LANES, LANES), const)],
        out_specs=pl.BlockSpec((ROW_BLOCK, width), lambda b, t: (b * nb + t, 0)),
        out_shape=jax.ShapeDtypeStruct((batch * tp, width), BF16),
        scratch_shapes=[pltpu.VMEM((width // LANES, LANES, LANES), F32),
                        pltpu.VMEM((ROW_BLOCK, width), F32),
                        pltpu.VMEM((8, width), F32), pltpu.VMEM((8, width), F32),
                        pltpu.VMEM((8, width), F32), pltpu.VMEM((8, lw), F32)],
        compiler_params=_cparams(("parallel", "arbitrary")),
    )(*([p] * (3 * nblk)), p_lora, par, mu_lora, w2, a2, g2, seg)


def _largest_tile(m, candidates):
    for t in candidates:
        if m % t == 0:
            return t
    return ROW_BLOCK


def kernel(x, meta_tokens, norm_mix, w_in, conv_w, attn_sinks, rwkv_mu, rwkv_w0, rwkv_w2, rwkv_a0, rwkv_a2, rwkv_g2, rwkv_k_k, rwkv_k_a, rwkv_r_k, rwkv_ln_w, rwkv_ln_b, w_out, norm_ffn, w_ffn_in, w_ffn_out, norm_final):
    batch, seq, d = x.shape
    depth = w_in.shape[0]
    conv_width = conv_w.shape[1]
    attn_width = ATTN_HEADS * HEAD_DIM
    kv_width = ATTN_KV_HEADS * HEAD_DIM
    rwkv_width = rwkv_w0.shape[1]
    ffn_hidden = w_ffn_out.shape[1]
    assert seq % ROW_BLOCK == 0
    pad = ROW_BLOCK - N_META
    tp = seq + ROW_BLOCK
    nb = tp // ROW_BLOCK
    m = batch * tp

    lora_w = 1024
    gate_pad = 512
    col_conv = 0
    col_q = CONV_K * conv_width
    col_k = col_q + attn_width
    col_v = col_k + kv_width
    col_rwkv = col_v + kv_width
    col_lora = col_rwkv + 3 * rwkv_width

    tm = _largest_tile(m, (1280,))
    tm_swiglu = _largest_tile(m, (1664, 1280))
    tm_ffn_out = _largest_tile(m, (640,))
    tt_conv = _largest_tile(tp, (640,))

    w_in_g = w_in * norm_mix[:, :, None]
    w_main = w_in_g.astype(BF16)
    w_lora = jnp.pad(w_in_g[:, :, col_lora:],
                     ((0, 0), (0, 0), (0, lora_w - (w_in.shape[2] - col_lora)))).astype(BF16)
    w_out_b = w_out.astype(BF16)
    w_ffn_in_b = (w_ffn_in * norm_ffn[:, :, None]).astype(BF16)
    w_ffn_out_b = w_ffn_out.astype(BF16)

    h, hb, ssq = embed(x.reshape(batch * seq, d), meta_tokens.astype(x.dtype), batch, nb)
    for l in range(depth):
        mu = rwkv_mu[l]
        par = jnp.zeros((16, rwkv_width), F32)
        par = par.at[0].set(mu[0:rwkv_width])
        par = par.at[1].set(mu[rwkv_width:2 * rwkv_width])
        par = par.at[2].set(mu[2 * rwkv_width:3 * rwkv_width])
        par = par.at[3].set(rwkv_w0[l]).at[4].set(rwkv_a0[l]).at[5].set(rwkv_k_k[l])
        par = par.at[6].set(rwkv_k_a[l]).at[7].set(rwkv_r_k[l])
        par = par.at[8].set(rwkv_ln_w[l]).at[9].set(rwkv_ln_b[l])
        mu_lora = jnp.pad(mu[3 * rwkv_width:].reshape(1, -1),
                          ((0, 0), (0, lora_w - (mu.shape[0] - 3 * rwkv_width))))
        g2 = jnp.pad(rwkv_g2[l], ((0, gate_pad - rwkv_g2.shape[1]), (0, 0))).astype(BF16)

        p = normed_matmul(hb, ssq, w_main, l, col_lora, BF16, tm, 1024)
        p_lora = normed_matmul(hb, ssq, w_lora, l, lora_w, F32, tm, 512)
        y_conv = conv_mixer(p, conv_w[l], batch, tp, pad, col_conv, conv_width, tt_conv)
        y_attn = swa_mixer(p, attn_sinks[l], batch, tp, pad, col_q, col_k, col_v)
        y_rwkv = rwkv_mixer(p, p_lora, par, mu_lora, rwkv_w2[l].astype(BF16),
                            rwkv_a2[l].astype(BF16), g2, batch, tp, pad, col_rwkv, rwkv_width)
        h, hb, ssq = _residual_call(_mix_out_kernel, [y_conv, y_attn, y_rwkv], w_out_b, l, h,
                                    tm, 512)
        act = normed_swiglu(hb, ssq, w_ffn_in_b, l, ffn_hidden, tm_swiglu, 256)
        h, hb, ssq = _residual_call(_mm_res_kernel, [act], w_ffn_out_b, l, h, tm_ffn_out, 256)

    out = final_norm(h, norm_final, batch, nb, x.dtype)
    return out.reshape(batch, seq, d)
```

```python
import functools
import math

import numpy as np
import jax
import jax.numpy as jnp
from jax import lax
from jax.experimental import pallas as pl
from jax.experimental.pallas import tpu as pltpu

F32 = jnp.float32
BF16 = jnp.bfloat16

N_META = 16
NORM_EPS = 1e-5
GN_EPS = 64e-5
CONV_K = 3
HEAD_DIM = 128
ATTN_HEADS = 12
ATTN_KV_HEADS = 4
ATTN_GROUP = ATTN_HEADS // ATTN_KV_HEADS
WINDOW = 128
RWKV_N = 64
DECAY_RANK = 128
ICLR_RANK = 128
GATE_RANK = 480

LANES = 128
ROW_BLOCK = 128
CHUNK = 64
RWKV_COLS = 512
ROW_CHUNK = (416, 320)
VMEM_LIMIT = 56 * 1024 * 1024
VMEM_LIMIT_HIGH = 62 * 1024 * 1024
NEG = -1e30


def _alibi_slopes(n):
    def pow2_slopes(m):
        start = 2.0 ** (-8.0 / m)
        return [start ** (i + 1) for i in range(m)]
    c = 2 ** int(math.floor(math.log2(n)))
    s = pow2_slopes(c)
    if c < n:
        s = s + pow2_slopes(2 * c)[0::2][: n - c]
    return [float(np.float32(v)) for v in s]


def _cparams(sem):
    return pltpu.CompilerParams(dimension_semantics=sem, vmem_limit_bytes=VMEM_LIMIT)


def _dot(a, b):
    return jnp.dot(a, b, preferred_element_type=F32)


def _dot_nt(a, b):
    return lax.dot_general(a, b, (((1,), (1,)), ((), ())), preferred_element_type=F32)


def _dot_tn(a, b):
    return lax.dot_general(a, b, (((0,), (0,)), ((), ())), preferred_element_type=F32)


def _sigmoid(x):
    return 1.0 / (1.0 + jnp.exp(-x))


def _lane_partial_sumsq(x):
    sq = x * x
    acc = sq[:, 0:LANES]
    for c in range(1, x.shape[1] // LANES):
        acc = acc + sq[:, c * LANES:(c + 1) * LANES]
    return acc


def _row_scale(ssq_ref, d):
    return lax.rsqrt(jnp.sum(ssq_ref[...], axis=-1, keepdims=True) * (1.0 / d) + NORM_EPS)


def _embed_kernel(x_ref, meta_ref, h_ref, hb_ref, ssq_ref):
    pad = h_ref.shape[0] - meta_ref.shape[0]

    @pl.when(pl.program_id(1) == 0)
    def _():
        h_ref[0:pad, :] = jnp.zeros((pad, h_ref.shape[1]), h_ref.dtype)
        h_ref[pad:, :] = meta_ref[...]

    @pl.when(pl.program_id(1) > 0)
    def _():
        h_ref[...] = x_ref[...]

    h = h_ref[...]
    hb_ref[...] = h.astype(hb_ref.dtype)
    ssq_ref[...] = _lane_partial_sumsq(h)


def embed(x2, meta_tokens, batch, nb):
    d = x2.shape[1]
    nx = nb - 1
    m = batch * nb * ROW_BLOCK
    blk = pl.BlockSpec((ROW_BLOCK, d), lambda b, i: (b * nb + i, 0))
    return pl.pallas_call(
        _embed_kernel,
        grid=(batch, nb),
        in_specs=[pl.BlockSpec((ROW_BLOCK, d), lambda b, i: (b * nx + jnp.maximum(i - 1, 0), 0)),
                  pl.BlockSpec((N_META, d), lambda b, i: (0, 0))],
        out_specs=[blk, blk, pl.BlockSpec((ROW_BLOCK, LANES), lambda b, i: (b * nb + i, 0))],
        out_shape=[jax.ShapeDtypeStruct((m, d), F32),
                   jax.ShapeDtypeStruct((m, d), BF16),
                   jax.ShapeDtypeStruct((m, LANES), F32)],
        compiler_params=_cparams(("parallel", "arbitrary")),
    )(x2, meta_tokens)


def _final_norm_kernel(h_ref, g_ref, o_ref):
    x = h_ref[...]
    ms = jnp.mean(x * x, axis=-1, keepdims=True)
    o_ref[...] = (x * lax.rsqrt(ms + NORM_EPS) * g_ref[...]).astype(o_ref.dtype)


def final_norm(h, g, batch, nb, out_dtype):
    d = h.shape[1]
    nx = nb - 1
    return pl.pallas_call(
        _final_norm_kernel,
        grid=(batch, nx),
        in_specs=[pl.BlockSpec((ROW_BLOCK, d), lambda b, i: (b * nb + i + 1, 0)),
                  pl.BlockSpec((1, d), lambda b, i: (0, 0))],
        out_specs=pl.BlockSpec((ROW_BLOCK, d), lambda b, i: (b * nx + i, 0)),
        out_shape=jax.ShapeDtypeStruct((batch * nx * ROW_BLOCK, d), out_dtype),
        compiler_params=_cparams(("parallel", "parallel")),
    )(h, g.reshape(1, d))


def _normed_mm_kernel(a_ref, ssq_ref, wt_ref, o_ref):
    s = _row_scale(ssq_ref, a_ref.shape[1])
    o_ref[...] = (_dot_nt(a_ref[...], wt_ref[...]) * s).astype(o_ref.dtype)


def normed_matmul(hb, ssq, wt, layer, n, out_dtype, tm, tn):
    m, k = hb.shape
    tn = min(tn, n)
    w_spec = pl.BlockSpec((None, tn, k), lambda i, j: (layer, j, 0))
    return pl.pallas_call(
        _normed_mm_kernel,
        grid=(m // tm, n // tn),
        in_specs=[pl.BlockSpec((tm, k), lambda i, j: (i, 0)),
                  pl.BlockSpec((tm, LANES), lambda i, j: (i, 0)),
                  w_spec],
        out_specs=pl.BlockSpec((tm, tn), lambda i, j: (i, j)),
        out_shape=jax.ShapeDtypeStruct((m, n), out_dtype),
        compiler_params=_cparams(("parallel", "arbitrary")),
    )(hb, ssq, wt)


def _normed_swiglu_kernel(a_ref, ssq_ref, wg_ref, wu_ref, o_ref):
    d = a_ref.shape[1]
    for rows in _row_chunks(o_ref.shape[0]):
        s = lax.rsqrt(jnp.sum(ssq_ref[rows, :], axis=-1, keepdims=True) * (1.0 / d) + NORM_EPS)
        a = a_ref[rows, :]
        g = _dot(a, wg_ref[...]) * s
        u = _dot(a, wu_ref[...]) * s
        o_ref[rows, :] = (g * _sigmoid(g) * u).astype(o_ref.dtype)


def normed_swiglu(hb, ssq, w, layer, hidden, tm, tn):
    m, k = hb.shape
    nt = hidden // tn
    assert hidden % tn == 0
    return pl.pallas_call(
        _normed_swiglu_kernel,
        grid=(m // tm, nt),
        in_specs=[pl.BlockSpec((tm, k), lambda i, j: (i, 0)),
                  pl.BlockSpec((tm, LANES), lambda i, j: (i, 0)),
                  pl.BlockSpec((None, k, tn), lambda i, j: (layer, 0, j)),
                  pl.BlockSpec((None, k, tn), lambda i, j: (layer, 0, j + nt))],
        out_specs=pl.BlockSpec((tm, tn), lambda i, j: (i, j)),
        out_shape=jax.ShapeDtypeStruct((m, hidden), BF16),
        compiler_params=_cparams(("parallel", "arbitrary")),
    )(hb, ssq, w, w)


def _row_chunks(tm):
    step = next((c for c in ROW_CHUNK if tm % c == 0), tm)
    return [slice(r, r + step) for r in range(0, tm, step)]


def _emit_residual(acc, rows, r_ref, h_ref, hb_ref, ssq_ref):
    h = r_ref[rows, :] + acc
    h_ref[rows, :] = h
    hb_ref[rows, :] = h.astype(hb_ref.dtype)
    ssq_ref[rows, :] += _lane_partial_sumsq(h)


def _zero_ssq_on_first_column_tile(ssq_ref):
    @pl.when(pl.program_id(1) == 0)
    def _():
        ssq_ref[...] = jnp.zeros_like(ssq_ref)


def _mix_out_kernel(a1_ref, a2_ref, a3_ref, w_ref, r_ref, h_ref, hb_ref, ssq_ref):
    k1, k2 = a1_ref.shape[1], a2_ref.shape[1]
    _zero_ssq_on_first_column_tile(ssq_ref)
    for rows in _row_chunks(h_ref.shape[0]):
        acc = _dot(a1_ref[rows, :], w_ref[0:k1, :])
        acc = acc + _dot(a2_ref[rows, :], w_ref[k1:k1 + k2, :])
        acc = acc + _dot(a3_ref[rows, :], w_ref[k1 + k2:, :])
        _emit_residual(acc, rows, r_ref, h_ref, hb_ref, ssq_ref)


def _mm_res_kernel(a_ref, w_ref, r_ref, h_ref, hb_ref, ssq_ref):
    _zero_ssq_on_first_column_tile(ssq_ref)
    for rows in _row_chunks(h_ref.shape[0]):
        _emit_residual(_dot(a_ref[rows, :], w_ref[...]), rows, r_ref, h_ref, hb_ref, ssq_ref)


def _residual_call(body, acts, w, layer, res, tm, tn, scratch=(), vmem_limit=VMEM_LIMIT):
    m, n = res.shape
    k = w.shape[1]
    tn = min(tn, n)
    tile = pl.BlockSpec((tm, tn), lambda i, j: (i, j))
    return pl.pallas_call(
        body,
        grid=(m // tm, n // tn),
        in_specs=[pl.BlockSpec((tm, a.shape[1]), lambda i, j: (i, 0)) for a in acts] + [
            pl.BlockSpec((None, k, tn), lambda i, j: (layer, 0, j)), tile],
        out_specs=[tile, tile, pl.BlockSpec((tm, LANES), lambda i, j: (i, 0))],
        out_shape=[jax.ShapeDtypeStruct((m, n), F32),
                   jax.ShapeDtypeStruct((m, n), BF16),
                   jax.ShapeDtypeStruct((m, LANES), F32)],
        scratch_shapes=list(scratch),
        compiler_params=pltpu.CompilerParams(dimension_semantics=("parallel", "arbitrary"),
                                             vmem_limit_bytes=vmem_limit),
    )(*acts, w, res)


def _conv_kernel(cb_ref, cc_ref, ch_ref, w_ref, o_ref, carry_ref, *, pad):
    first = pl.program_id(1) == 0

    @pl.when(first)
    def _():
        carry_ref[...] = jnp.zeros_like(carry_ref)

    u = cc_ref[...].astype(F32) * ch_ref[...].astype(F32)
    tt = u.shape[0]
    row = lax.broadcasted_iota(jnp.int32, u.shape, 0)
    u = jnp.where((row >= pad) | jnp.logical_not(first), u, 0.0)
    c0 = carry_ref[0:1, :]
    c1 = carry_ref[1:2, :]
    u1 = jnp.where(row == 0, c1, pltpu.roll(u, 1, axis=0))
    u2 = jnp.where(row == 0, c0, jnp.where(row == 1, c1, pltpu.roll(u, 2, axis=0)))
    carry_ref[0:2, :] = u[tt - 2:tt, :]
    y = u2 * w_ref[0:1, :]
    y = y + u1 * w_ref[1:2, :]
    y = y + u * w_ref[2:3, :]
    o_ref[...] = (cb_ref[...].astype(F32) * y).astype(o_ref.dtype)


def conv_mixer(p, conv_w, batch, tp, pad, col0, width, tt):
    nt = tp // tt
    assert pad <= tt and col0 % width == 0
    cb = col0 // width
    w = jnp.zeros((8, width), F32).at[0:CONV_K].set(conv_w.T)
    return pl.pallas_call(
        functools.partial(_conv_kernel, pad=pad),
        grid=(batch, nt),
        in_specs=[pl.BlockSpec((tt, width), lambda b, t: (b * nt + t, cb)),
                  pl.BlockSpec((tt, width), lambda b, t: (b * nt + t, cb + 1)),
                  pl.BlockSpec((tt, width), lambda b, t: (b * nt + t, cb + 2)),
                  pl.BlockSpec((8, width), lambda b, t: (0, 0))],
        out_specs=pl.BlockSpec((tt, width), lambda b, t: (b * nt + t, 0)),
        out_shape=jax.ShapeDtypeStruct((batch * tp, width), BF16),
        scratch_shapes=[pltpu.VMEM((8, width), F32)],
        compiler_params=_cparams(("parallel", "arbitrary")),
    )(p, p, p, w)


def _swa_kernel(sink_ref, q_ref, kp_ref, kc_ref, km_ref, vp_ref, vc_ref, vm_ref, o_ref, *,
                slopes, pad):
    j = pl.program_id(1)
    blk = ROW_BLOCK
    g = ATTN_GROUP
    scale = HEAD_DIM ** -0.5
    rows = g * blk
    ri = lax.broadcasted_iota(jnp.int32, (rows, 3 * blk), 0)
    ci = lax.broadcasted_iota(jnp.int32, (rows, 3 * blk), 1)
    q_pos = j * blk + ri % blk - pad
    k_pos = jnp.where(ci < blk, ci, (j - 2) * blk + ci) - pad
    dist = q_pos - k_pos
    is_meta = ci < blk
    ok = (is_meta & (k_pos >= 0) & (k_pos <= q_pos)) | (
        (ci >= blk) & (dist >= 0) & (dist < WINDOW) & (k_pos >= N_META))
    distf = jnp.where(is_meta, 0, dist).astype(F32)
    rgrp = lax.broadcasted_iota(jnp.int32, (rows, 1), 0) // blk
    for h in range(ATTN_KV_HEADS):
        sl = slice(h * HEAD_DIM, (h + 1) * HEAD_DIM)
        kb = jnp.concatenate([km_ref[:, sl], kp_ref[:, sl], kc_ref[:, sl]], axis=0).astype(BF16)
        vb = jnp.concatenate([vm_ref[:, sl], vp_ref[:, sl], vc_ref[:, sl]], axis=0).astype(BF16)
        q3 = jnp.concatenate(
            [q_ref[:, (h * g + i) * HEAD_DIM:(h * g + i + 1) * HEAD_DIM] for i in range(g)],
            axis=0).astype(BF16)
        slope = jnp.full((rows, 1), slopes[h * g], F32)
        sink = jnp.full((rows, 1), sink_ref[h * g], F32)
        for i in range(1, g):
            slope = jnp.where(rgrp == i, slopes[h * g + i], slope)
            sink = jnp.where(rgrp == i, sink_ref[h * g + i], sink)
        s = _dot_nt(q3, kb) * scale - slope * distf
        s = jnp.where(ok, s, NEG)
        mx = jnp.maximum(jnp.max(s, axis=-1, keepdims=True), sink)
        e = jnp.exp(s - mx)
        den = jnp.sum(e, axis=-1, keepdims=True) + jnp.exp(sink - mx)
        o = _dot(e.astype(BF16), vb) / den
        for i in range(g):
            o_ref[:, (h * g + i) * HEAD_DIM:(h * g + i + 1) * HEAD_DIM] = (
                o[i * blk:(i + 1) * blk, :].astype(o_ref.dtype))


def swa_mixer(p, sinks, batch, tp, pad, col_q, col_k, col_v):
    nb = tp // ROW_BLOCK
    qw = ATTN_HEADS * HEAD_DIM
    kw = ATTN_KV_HEADS * HEAD_DIM
    assert col_q % qw == 0 and col_k % kw == 0 and col_v % kw == 0 and pad + N_META == ROW_BLOCK
    qc, kc, vc = col_q // qw, col_k // kw, col_v // kw
    cur = lambda c: (lambda b, j, s: (b * nb + j, c))
    prev = lambda c: (lambda b, j, s: (b * nb + jnp.maximum(j - 1, 0), c))
    meta = lambda c: (lambda b, j, s: (b * nb, c))
    grid_spec = pltpu.PrefetchScalarGridSpec(
        num_scalar_prefetch=1,
        grid=(batch, nb),
        in_specs=[pl.BlockSpec((ROW_BLOCK, qw), cur(qc)),
                  pl.BlockSpec((ROW_BLOCK, kw), prev(kc)),
                  pl.BlockSpec((ROW_BLOCK, kw), cur(kc)),
                  pl.BlockSpec((ROW_BLOCK, kw), meta(kc)),
                  pl.BlockSpec((ROW_BLOCK, kw), prev(vc)),
                  pl.BlockSpec((ROW_BLOCK, kw), cur(vc)),
                  pl.BlockSpec((ROW_BLOCK, kw), meta(vc))],
        out_specs=pl.BlockSpec((ROW_BLOCK, qw), lambda b, j, s: (b * nb + j, 0)),
    )
    return pl.pallas_call(
        functools.partial(_swa_kernel, slopes=_alibi_slopes(ATTN_HEADS), pad=pad),
        grid_spec=grid_spec,
        out_shape=jax.ShapeDtypeStruct((batch * tp, qw), BF16),
        compiler_params=_cparams(("parallel", "arbitrary")),
    )(sinks.astype(F32), p, p, p, p, p, p, p)


def _split3(x):
    p1 = x.astype(BF16)
    r1 = x - p1.astype(F32)
    p2 = r1.astype(BF16)
    p3 = (r1 - p2.astype(F32)).astype(BF16)
    return p1, p2, p3


def _rwkv_kernel(*refs, pad, nblk):
    pr_refs = refs[0:nblk]
    pk_refs = refs[nblk:2 * nblk]
    pv_refs = refs[2 * nblk:3 * nblk]
    (pl_ref, par_ref, mul_ref, w2_ref, a2_ref, g2_ref, seg_ref, o_ref,
     s_ref, y_ref, cr_ref, ck_ref, cv_ref, cl_ref) = refs[3 * nblk:]
    tb, width = o_ref.shape
    npair = width // LANES
    c = CHUNK
    nch = tb // c
    first = pl.program_id(1) == 0

    @pl.when(first)
    def _():
        s_ref[...] = jnp.zeros_like(s_ref)
        cr_ref[...] = jnp.zeros_like(cr_ref)
        ck_ref[...] = jnp.zeros_like(ck_ref)
        cv_ref[...] = jnp.zeros_like(cv_ref)
        cl_ref[...] = jnp.zeros_like(cl_ref)

    row1 = lax.broadcasted_iota(jnp.int32, (tb, 1), 0)
    valid = (row1 >= pad) | jnp.logical_not(first)

    def shift_mix(x, c_ref, mu):
        x = jnp.where(valid, x.astype(F32), 0.0)
        prev = jnp.where(row1 == 0, c_ref[0:1, :], pltpu.roll(x, 1, axis=0))
        c_ref[0:1, :] = x[tb - 1:tb, :]
        return x + (prev - x) * mu

    def wide(block_refs):
        return jnp.concatenate([b[...] for b in block_refs], axis=1)

    par = par_ref[...]
    mu_r, mu_k, mu_v = par[0:1], par[1:2], par[2:3]
    w0, a0, k_k, k_a, r_k = par[3:4], par[4:5], par[5:6], par[6:7], par[7:8]
    ln_w, ln_b = par[8:9], par[9:10]

    r = shift_mix(wide(pr_refs), cr_ref, mu_r)
    k = shift_mix(wide(pk_refs), ck_ref, mu_k)
    v = shift_mix(wide(pv_refs), cv_ref, mu_v)
    lo = shift_mix(pl_ref[...], cl_ref, mul_ref[...])
    wl = lo[:, 0:DECAY_RANK]
    al = lo[:, DECAY_RANK:DECAY_RANK + ICLR_RANK]
    gl = lo[:, DECAY_RANK + ICLR_RANK:DECAY_RANK + ICLR_RANK + g2_ref.shape[0]]

    seg = seg_ref[...]

    def seg_sum(x):
        xs = jnp.concatenate([x[:, i * LANES:(i + 1) * LANES] for i in range(npair)], axis=0)
        hi = xs.astype(BF16)
        lo_ = (xs - hi.astype(F32)).astype(BF16)
        ss = _dot(jnp.concatenate([hi, lo_], axis=1), seg)
        return jnp.concatenate([ss[i * tb:(i + 1) * tb, :] for i in range(npair)], axis=1)

    z = -(w0 + _dot(jnp.tanh(wl).astype(BF16), w2_ref[...]))
    w = -(jnp.maximum(z, 0.0) + jnp.log(1.0 + jnp.exp(-jnp.abs(z)))) - 0.5
    lw = -jnp.exp(w)
    a = _sigmoid(a0 + _dot(al.astype(BF16), a2_ref[...]))
    gate = _dot(_sigmoid(gl).astype(BF16), g2_ref[...])

    kk = k * k_k
    kk = kk / jnp.maximum(jnp.sqrt(seg_sum(kk * kk)), 1e-12)
    k2 = k * (1.0 + (a - 1.0) * k_a)
    bvec = kk * a
    bonus = seg_sum(r * k2 * r_k) * v

    ti = lax.broadcasted_iota(jnp.int32, (tb, tb), 0)
    tj = lax.broadcasted_iota(jnp.int32, (tb, tb), 1)
    tri = jnp.where((ti // c == tj // c) & (tj <= ti), 1.0, 0.0).astype(BF16)
    cum = _dot(jnp.concatenate([tri, tri, tri], axis=1), jnp.concatenate(_split3(lw), axis=0))

    r_hat = r * jnp.exp(cum)
    a_hat = -kk * jnp.exp(cum - lw)
    dinv = jnp.exp(-cum)
    b_til = bvec * dinv
    k_til = k2 * dinv

    ri = lax.broadcasted_iota(jnp.int32, (2 * c, 2 * c), 0)
    ci = lax.broadcasted_iota(jnp.int32, (2 * c, 2 * c), 1)
    head_mask = jnp.where((ri // c) == (ci // c), 1.0, 0.0).astype(BF16)

    def dup(xb):
        return jnp.concatenate([xb, xb], axis=0) * head_mask

    rt = lax.broadcasted_iota(jnp.int32, (c, 2 * c), 0)
    cs = lax.broadcasted_iota(jnp.int32, (c, 2 * c), 1) % c
    strict = cs < rt
    incl = cs <= rt
    eye = jnp.where(cs == rt, 1.0, 0.0)

    units = [(ch, pr) for ch in range(nch) for pr in range(npair)]

    def sl(u):
        ch, pr = u
        return slice(ch * c, (ch + 1) * c), slice(pr * LANES, (pr + 1) * LANES)

    a_s, gr, v_s, bk, d_tot, sc = {}, {}, {}, {}, {}, {}
    for u in units:
        rs, ls = sl(u)
        last = rs.stop - 1
        d_tot[u] = jnp.exp(cum[last:last + 1, ls])
        a_u = a_hat[rs, ls].astype(BF16)
        gr[u] = r_hat[rs, ls].astype(BF16)
        a_s[u] = dup(a_u)
        v_s[u] = dup(v[rs, ls].astype(BF16))
        bt = b_til[rs, ls]
        kt = k_til[rs, ls]
        bk[u] = jnp.concatenate([dup((bt * d_tot[u]).astype(BF16)),
                                 dup((kt * d_tot[u]).astype(BF16))], axis=0)
        rhs = jnp.concatenate([dup(bt.astype(BF16)), dup(kt.astype(BF16))], axis=0)
        sc[u] = _dot_nt(jnp.concatenate([a_u, gr[u]], axis=0), rhs)

    lp, tinv, m_r, mak_v = {}, {}, {}, {}
    for u in units:
        l_ab = jnp.where(strict, sc[u][0:c, 0:2 * c], 0.0)
        m_ak = jnp.where(strict, sc[u][0:c, 2 * c:4 * c], 0.0).astype(BF16)
        m_r[u] = jnp.concatenate([jnp.where(incl, sc[u][c:2 * c, 0:2 * c], 0.0),
                                  jnp.where(incl, sc[u][c:2 * c, 2 * c:4 * c], 0.0)],
                                 axis=1).astype(BF16)
        lp[u] = l_ab.astype(BF16)
        tinv[u] = eye + l_ab
        mak_v[u] = _dot(m_ak, v_s[u]).astype(BF16)
    del sc

    nlev = int(math.log2(c))
    for u in units:
        lp[u] = _dot(lp[u], dup(lp[u])).astype(BF16)
    for lev in range(1, nlev - 1):
        for u in units:
            res = _dot(lp[u], jnp.concatenate([dup(lp[u]), dup(tinv[u].astype(BF16))], axis=1))
            tinv[u] = tinv[u] + res[:, 2 * c:4 * c]
            lp[u] = res[:, 0:2 * c].astype(BF16)
    for u in units:
        tinv[u] = tinv[u] + _dot(lp[u], dup(tinv[u].astype(BF16)))

    u_loc = {}
    for u in units:
        gu = _dot(tinv[u].astype(BF16), jnp.concatenate([a_s[u], dup(mak_v[u])], axis=1))
        gr[u] = jnp.concatenate([gu[:, 0:LANES].astype(BF16), gr[u]], axis=0)
        u_loc[u] = gu[:, LANES:2 * LANES]

    for ch in range(nch):
        xs = {}
        for pr in range(npair):
            u = (ch, pr)
            state = s_ref[pr]
            xs[u] = _dot_nt(gr[u], state.astype(BF16))
        for pr in range(npair):
            u = (ch, pr)
            rs, ls = sl(u)
            us = (xs[u][0:c] + u_loc[u]).astype(BF16)
            uv = jnp.concatenate([dup(us), v_s[u]], axis=0)
            y_ref[rs, ls] = xs[u][c:2 * c] + _dot(m_r[u], uv)
            s_ref[pr] = s_ref[pr] * d_tot[u] + _dot_tn(uv, bk[u])

    y = y_ref[...]
    inv_n = 1.0 / RWKV_N
    mean = seg_sum(y) * inv_n
    yc = y - mean
    var = seg_sum(yc * yc) * inv_n
    yn = yc * lax.rsqrt(var + GN_EPS) * ln_w + ln_b
    o_ref[...] = ((yn + bonus) * gate).astype(o_ref.dtype)


def rwkv_mixer(p, p_lora, par, mu_lora, w2, a2, g2, batch, tp, pad, col0, width):
    nb = tp // ROW_BLOCK
    lw = mu_lora.shape[1]
    cw = RWKV_COLS
    assert col0 % cw == 0 and width % cw == 0 and pad < ROW_BLOCK and p_lora.shape[1] == lw
    nblk = width // cw
    c0 = col0 // cw
    seg = (np.arange(LANES)[:, None] // RWKV_N == np.arange(LANES)[None, :] // RWKV_N)
    seg = jnp.asarray(np.concatenate([seg, seg], axis=0), BF16)
    const = lambda b, t: (0, 0)

    def col_block(cidx):
        return pl.BlockSpec((ROW_BLOCK, cw), lambda b, t: (b * nb + t, cidx))

    return pl.pallas_call(
        functools.partial(_rwkv_kernel, pad=pad, nblk=nblk),
        grid=(batch, nb),
        in_specs=[col_block(c0 + i) for i in range(3 * nblk)] + [
            pl.BlockSpec((ROW_BLOCK, lw), lambda b, t: (b * nb + t, 0)),
            pl.BlockSpec((16, width), const),
            pl.BlockSpec((1, lw), const),
            pl.BlockSpec((DECAY_RANK, width), const),
            pl.BlockSpec((ICLR_RANK, width), const),
            pl.BlockSpec((g2.shape[0], width), const),
            pl.BlockSpec((2 * LANES, LANES), const)],
        out_specs=pl.BlockSpec((ROW_BLOCK, width), lambda b, t: (b * nb + t, 0)),
        out_shape=jax.ShapeDtypeStruct((batch * tp, width), BF16),
        scratch_shapes=[pltpu.VMEM((width // LANES, LANES, LANES), F32),
                        pltpu.VMEM((ROW_BLOCK, width), F32),
                        pltpu.VMEM((8, width), F32), pltpu.VMEM((8, width), F32),
                        pltpu.VMEM((8, width), F32), pltpu.VMEM((8, lw), F32)],
        compiler_params=_cparams(("parallel", "arbitrary")),
    )(*([p] * (3 * nblk)), p_lora, par, mu_lora, w2, a2, g2, seg)


def _largest_tile(m, candidates):
    for t in candidates:
        if m % t == 0:
            return t
    return ROW_BLOCK


def kernel(x, meta_tokens, norm_mix, w_in, conv_w, attn_sinks, rwkv_mu, rwkv_w0, rwkv_w2, rwkv_a0, rwkv_a2, rwkv_g2, rwkv_k_k, rwkv_k_a, rwkv_r_k, rwkv_ln_w, rwkv_ln_b, w_out, norm_ffn, w_ffn_in, w_ffn_out, norm_final):
    batch, seq, d = x.shape
    depth = w_in.shape[0]
    conv_width = conv_w.shape[1]
    attn_width = ATTN_HEADS * HEAD_DIM
    kv_width = ATTN_KV_HEADS * HEAD_DIM
    rwkv_width = rwkv_w0.shape[1]
    ffn_hidden = w_ffn_out.shape[1]
    assert seq % ROW_BLOCK == 0
    pad = ROW_BLOCK - N_META
    tp = seq + ROW_BLOCK
    nb = tp // ROW_BLOCK
    m = batch * tp

    lora_w = 1024
    gate_pad = 512
    col_conv = 0
    col_q = CONV_K * conv_width
    col_k = col_q + attn_width
    col_v = col_k + kv_width
    col_rwkv = col_v + kv_width
    col_lora = col_rwkv + 3 * rwkv_width

    tm = _largest_tile(m, (1280,))
    tm_swiglu = _largest_tile(m, (2080, 1664, 1280))
    tm_ffn_out = _largest_tile(m, (640,))
    tt_conv = _largest_tile(tp, (640,))

    w_main = (jnp.swapaxes(w_in, 1, 2) * norm_mix[:, None, :]).astype(BF16)
    w_lora = jnp.pad(w_main[:, col_lora:, :],
                     ((0, 0), (0, lora_w - (w_in.shape[2] - col_lora)), (0, 0)))
    w_out_b = w_out.astype(BF16)
    w_ffn_in_b = (w_ffn_in * norm_ffn[:, :, None]).astype(BF16)
    w_ffn_out_b = w_ffn_out.astype(BF16)

    h, hb, ssq = embed(x.reshape(batch * seq, d), meta_tokens.astype(x.dtype), batch, nb)
    for l in range(depth):
        mu = rwkv_mu[l]
        par = jnp.zeros((16, rwkv_width), F32)
        par = par.at[0].set(mu[0:rwkv_width])
        par = par.at[1].set(mu[rwkv_width:2 * rwkv_width])
        par = par.at[2].set(mu[2 * rwkv_width:3 * rwkv_width])
        par = par.at[3].set(rwkv_w0[l]).at[4].set(rwkv_a0[l]).at[5].set(rwkv_k_k[l])
        par = par.at[6].set(rwkv_k_a[l]).at[7].set(rwkv_r_k[l])
        par = par.at[8].set(rwkv_ln_w[l]).at[9].set(rwkv_ln_b[l])
        mu_lora = jnp.pad(mu[3 * rwkv_width:].reshape(1, -1),
                          ((0, 0), (0, lora_w - (mu.shape[0] - 3 * rwkv_width))))
        g2 = jnp.pad(rwkv_g2[l], ((0, gate_pad - rwkv_g2.shape[1]), (0, 0))).astype(BF16)

        p = normed_matmul(hb, ssq, w_main, l, col_lora, BF16, tm, 1024)
        p_lora = normed_matmul(hb, ssq, w_lora, l, lora_w, F32, tm, 512)
        y_conv = conv_mixer(p, conv_w[l], batch, tp, pad, col_conv, conv_width, tt_conv)
        y_attn = swa_mixer(p, attn_sinks[l], batch, tp, pad, col_q, col_k, col_v)
        y_rwkv = rwkv_mixer(p, p_lora, par, mu_lora, rwkv_w2[l].astype(BF16),
                            rwkv_a2[l].astype(BF16), g2, batch, tp, pad, col_rwkv, rwkv_width)
        h, hb, ssq = _residual_call(_mix_out_kernel, [y_conv, y_attn, y_rwkv], w_out_b, l, h,
                                    tm, 512)
        act = normed_swiglu(hb, ssq, w_ffn_in_b, l, ffn_hidden, tm_swiglu, 256)
        h, hb, ssq = _residual_call(_mm_res_kernel, [act], w_ffn_out_b, l, h, tm_ffn_out, 512,
                                    vmem_limit=VMEM_LIMIT_HIGH)

    out = final_norm(h, norm_final, batch, nb, x.dtype)
    return out.reshape(batch, seq, d)
```

```python
import functools
import math
from typing import NamedTuple

import numpy as np
import jax
import jax.numpy as jnp
from jax import lax
from jax.experimental import pallas as pl
from jax.experimental.pallas import tpu as pltpu

F32 = jnp.float32
BF16 = jnp.bfloat16

N_META = 16
NORM_EPS = 1e-5
GN_EPS = 64e-5
CONV_K = 3
HEAD_DIM = 128
ATTN_HEADS = 12
ATTN_KV_HEADS = 4
ATTN_GROUP = ATTN_HEADS // ATTN_KV_HEADS
WINDOW = 128
RWKV_N = 64
DECAY_RANK = 128
ICLR_RANK = 128
GATE_RANK = 480
LORA_BLOCK = 1024

LANES = 128
ROW_BLOCK = 128
CHUNK = 64
RWKV_COLS = 512
ROW_CHUNK = (416, 320)
VMEM_LIMIT = 56 * 1024 * 1024
VMEM_LIMIT_HIGH = 62 * 1024 * 1024
NEG = -1e30


def _alibi_slopes(n):
    def pow2_slopes(m):
        start = 2.0 ** (-8.0 / m)
        return [start ** (i + 1) for i in range(m)]
    c = 2 ** int(math.floor(math.log2(n)))
    s = pow2_slopes(c)
    if c < n:
        s = s + pow2_slopes(2 * c)[0::2][: n - c]
    return [float(np.float32(v)) for v in s]


def _cparams(sem):
    return pltpu.CompilerParams(dimension_semantics=sem, vmem_limit_bytes=VMEM_LIMIT)


def _dot(a, b):
    return jnp.dot(a, b, preferred_element_type=F32)


def _dot_nt(a, b):
    return lax.dot_general(a, b, (((1,), (1,)), ((), ())), preferred_element_type=F32)


def _dot_tn(a, b):
    return lax.dot_general(a, b, (((0,), (0,)), ((), ())), preferred_element_type=F32)


def _sigmoid(x):
    return 1.0 / (1.0 + jnp.exp(-x))


def _lane_partial_sumsq(x):
    sq = x * x
    acc = sq[:, 0:LANES]
    for c in range(1, x.shape[1] // LANES):
        acc = acc + sq[:, c * LANES:(c + 1) * LANES]
    return acc


def _row_scale(ssq_ref, d):
    return lax.rsqrt(jnp.sum(ssq_ref[...], axis=-1, keepdims=True) * (1.0 / d) + NORM_EPS)


def _embed_kernel(x_ref, meta_ref, h_ref, hb_ref, ssq_ref):
    pad = h_ref.shape[0] - meta_ref.shape[0]

    @pl.when(pl.program_id(1) == 0)
    def _():
        h_ref[0:pad, :] = jnp.zeros((pad, h_ref.shape[1]), h_ref.dtype)
        h_ref[pad:, :] = meta_ref[...]

    @pl.when(pl.program_id(1) > 0)
    def _():
        h_ref[...] = x_ref[...]

    h = h_ref[...]
    hb_ref[...] = h.astype(hb_ref.dtype)
    ssq_ref[...] = _lane_partial_sumsq(h)


def embed(x2, meta_tokens, batch, nb):
    d = x2.shape[1]
    nx = nb - 1
    m = batch * nb * ROW_BLOCK
    blk = pl.BlockSpec((ROW_BLOCK, d), lambda b, i: (b * nb + i, 0))
    return pl.pallas_call(
        _embed_kernel,
        grid=(batch, nb),
        in_specs=[pl.BlockSpec((ROW_BLOCK, d), lambda b, i: (b * nx + jnp.maximum(i - 1, 0), 0)),
                  pl.BlockSpec((N_META, d), lambda b, i: (0, 0))],
        out_specs=[blk, blk, pl.BlockSpec((ROW_BLOCK, LANES), lambda b, i: (b * nb + i, 0))],
        out_shape=[jax.ShapeDtypeStruct((m, d), F32),
                   jax.ShapeDtypeStruct((m, d), BF16),
                   jax.ShapeDtypeStruct((m, LANES), F32)],
        compiler_params=_cparams(("parallel", "arbitrary")),
    )(x2, meta_tokens)


def _final_norm_kernel(h_ref, g_ref, o_ref):
    x = h_ref[...]
    ms = jnp.mean(x * x, axis=-1, keepdims=True)
    o_ref[...] = (x * lax.rsqrt(ms + NORM_EPS) * g_ref[...]).astype(o_ref.dtype)


def final_norm(h, g, batch, nb, out_dtype):
    d = h.shape[1]
    nx = nb - 1
    return pl.pallas_call(
        _final_norm_kernel,
        grid=(batch, nx),
        in_specs=[pl.BlockSpec((ROW_BLOCK, d), lambda b, i: (b * nb + i + 1, 0)),
                  pl.BlockSpec((1, d), lambda b, i: (0, 0))],
        out_specs=pl.BlockSpec((ROW_BLOCK, d), lambda b, i: (b * nx + i, 0)),
        out_shape=jax.ShapeDtypeStruct((batch * nx * ROW_BLOCK, d), out_dtype),
        compiler_params=_cparams(("parallel", "parallel")),
    )(h, g.reshape(1, d))


def _normed_mm_kernel(a_ref, ssq_ref, wt_ref, o_ref):
    s = _row_scale(ssq_ref, a_ref.shape[1])
    o_ref[...] = (_dot_nt(a_ref[...], wt_ref[...]) * s).astype(o_ref.dtype)


def normed_matmul(hb, ssq, wt, layer, n, out_dtype, tm, tn):
    m, k = hb.shape
    tn = min(tn, n)
    w_spec = pl.BlockSpec((None, tn, k), lambda i, j: (layer, j, 0))
    return pl.pallas_call(
        _normed_mm_kernel,
        grid=(m // tm, n // tn),
        in_specs=[pl.BlockSpec((tm, k), lambda i, j: (i, 0)),
                  pl.BlockSpec((tm, LANES), lambda i, j: (i, 0)),
                  w_spec],
        out_specs=pl.BlockSpec((tm, tn), lambda i, j: (i, j)),
        out_shape=jax.ShapeDtypeStruct((m, n), out_dtype),
        compiler_params=_cparams(("parallel", "arbitrary")),
    )(hb, ssq, wt)


def _normed_swiglu_kernel(a_ref, ssq_ref, wg_ref, wu_ref, o_ref):
    d = a_ref.shape[1]
    for rows in _row_chunks(o_ref.shape[0]):
        s = lax.rsqrt(jnp.sum(ssq_ref[rows, :], axis=-1, keepdims=True) * (1.0 / d) + NORM_EPS)
        a = a_ref[rows, :]
        g = _dot(a, wg_ref[...]) * s
        u = _dot(a, wu_ref[...]) * s
        o_ref[rows, :] = (g * _sigmoid(g) * u).astype(o_ref.dtype)


def normed_swiglu(hb, ssq, w, layer, hidden, tm, tn):
    m, k = hb.shape
    nt = hidden // tn
    assert hidden % tn == 0
    return pl.pallas_call(
        _normed_swiglu_kernel,
        grid=(m // tm, nt),
        in_specs=[pl.BlockSpec((tm, k), lambda i, j: (i, 0)),
                  pl.BlockSpec((tm, LANES), lambda i, j: (i, 0)),
                  pl.BlockSpec((None, k, tn), lambda i, j: (layer, 0, j)),
                  pl.BlockSpec((None, k, tn), lambda i, j: (layer, 0, j + nt))],
        out_specs=pl.BlockSpec((tm, tn), lambda i, j: (i, j)),
        out_shape=jax.ShapeDtypeStruct((m, hidden), BF16),
        compiler_params=_cparams(("parallel", "arbitrary")),
    )(hb, ssq, w, w)


def _row_chunks(tm):
    step = next((c for c in ROW_CHUNK if tm % c == 0), tm)
    return [slice(r, r + step) for r in range(0, tm, step)]


def _emit_residual(acc, rows, r_ref, h_ref, hb_ref, ssq_ref):
    h = r_ref[rows, :] + acc
    h_ref[rows, :] = h
    hb_ref[rows, :] = h.astype(hb_ref.dtype)
    ssq_ref[rows, :] += _lane_partial_sumsq(h)


def _zero_ssq_on_first_column_tile(ssq_ref):
    @pl.when(pl.program_id(1) == 0)
    def _():
        ssq_ref[...] = jnp.zeros_like(ssq_ref)


def _mix_out_kernel(a1_ref, a2_ref, a3_ref, w_ref, r_ref, h_ref, hb_ref, ssq_ref):
    k1, k2 = a1_ref.shape[1], a2_ref.shape[1]
    _zero_ssq_on_first_column_tile(ssq_ref)
    for rows in _row_chunks(h_ref.shape[0]):
        acc = _dot(a1_ref[rows, :], w_ref[0:k1, :])
        acc = acc + _dot(a2_ref[rows, :], w_ref[k1:k1 + k2, :])
        acc = acc + _dot(a3_ref[rows, :], w_ref[k1 + k2:, :])
        _emit_residual(acc, rows, r_ref, h_ref, hb_ref, ssq_ref)


def _mm_res_kernel(a_ref, w_ref, r_ref, h_ref, hb_ref, ssq_ref):
    _zero_ssq_on_first_column_tile(ssq_ref)
    for rows in _row_chunks(h_ref.shape[0]):
        _emit_residual(_dot(a_ref[rows, :], w_ref[...]), rows, r_ref, h_ref, hb_ref, ssq_ref)


def _residual_call(body, acts, w, layer, res, tm, tn, scratch=(), vmem_limit=VMEM_LIMIT):
    m, n = res.shape
    k = w.shape[1]
    tn = min(tn, n)
    tile = pl.BlockSpec((tm, tn), lambda i, j: (i, j))
    return pl.pallas_call(
        body,
        grid=(m // tm, n // tn),
        in_specs=[pl.BlockSpec((tm, a.shape[1]), lambda i, j: (i, 0)) for a in acts] + [
            pl.BlockSpec((None, k, tn), lambda i, j: (layer, 0, j)), tile],
        out_specs=[tile, tile, pl.BlockSpec((tm, LANES), lambda i, j: (i, 0))],
        out_shape=[jax.ShapeDtypeStruct((m, n), F32),
                   jax.ShapeDtypeStruct((m, n), BF16),
                   jax.ShapeDtypeStruct((m, LANES), F32)],
        scratch_shapes=list(scratch),
        compiler_params=pltpu.CompilerParams(dimension_semantics=("parallel", "arbitrary"),
                                             vmem_limit_bytes=vmem_limit),
    )(*acts, w, res)


def _conv_kernel(cb_ref, cc_ref, ch_ref, w_ref, o_ref, carry_ref, *, pad):
    first = pl.program_id(1) == 0

    @pl.when(first)
    def _():
        carry_ref[...] = jnp.zeros_like(carry_ref)

    u = cc_ref[...].astype(F32) * ch_ref[...].astype(F32)
    tt = u.shape[0]
    row = lax.broadcasted_iota(jnp.int32, u.shape, 0)
    u = jnp.where((row >= pad) | jnp.logical_not(first), u, 0.0)
    c0 = carry_ref[0:1, :]
    c1 = carry_ref[1:2, :]
    u1 = jnp.where(row == 0, c1, pltpu.roll(u, 1, axis=0))
    u2 = jnp.where(row == 0, c0, jnp.where(row == 1, c1, pltpu.roll(u, 2, axis=0)))
    carry_ref[0:2, :] = u[tt - 2:tt, :]
    y = u2 * w_ref[0:1, :]
    y = y + u1 * w_ref[1:2, :]
    y = y + u * w_ref[2:3, :]
    o_ref[...] = (cb_ref[...].astype(F32) * y).astype(o_ref.dtype)


def conv_mixer(p, conv_w, batch, tp, pad, col0, width, tt):
    nt = tp // tt
    assert pad <= tt and col0 % width == 0
    cb = col0 // width
    w = jnp.zeros((8, width), F32).at[0:CONV_K].set(conv_w.T)
    return pl.pallas_call(
        functools.partial(_conv_kernel, pad=pad),
        grid=(batch, nt),
        in_specs=[pl.BlockSpec((tt, width), lambda b, t: (b * nt + t, cb)),
                  pl.BlockSpec((tt, width), lambda b, t: (b * nt + t, cb + 1)),
                  pl.BlockSpec((tt, width), lambda b, t: (b * nt + t, cb + 2)),
                  pl.BlockSpec((8, width), lambda b, t: (0, 0))],
        out_specs=pl.BlockSpec((tt, width), lambda b, t: (b * nt + t, 0)),
        out_shape=jax.ShapeDtypeStruct((batch * tp, width), BF16),
        scratch_shapes=[pltpu.VMEM((8, width), F32)],
        compiler_params=_cparams(("parallel", "arbitrary")),
    )(p, p, p, w)


def _swa_kernel(sink_ref, q_ref, kp_ref, kc_ref, km_ref, vp_ref, vc_ref, vm_ref, o_ref, *,
                slopes, pad):
    j = pl.program_id(1)
    blk = ROW_BLOCK
    g = ATTN_GROUP
    scale = HEAD_DIM ** -0.5
    rows = g * blk
    ri = lax.broadcasted_iota(jnp.int32, (rows, 3 * blk), 0)
    ci = lax.broadcasted_iota(jnp.int32, (rows, 3 * blk), 1)
    q_pos = j * blk + ri % blk - pad
    k_pos = jnp.where(ci < blk, ci, (j - 2) * blk + ci) - pad
    dist = q_pos - k_pos
    is_meta = ci < blk
    ok = (is_meta & (k_pos >= 0) & (k_pos <= q_pos)) | (
        (ci >= blk) & (dist >= 0) & (dist < WINDOW) & (k_pos >= N_META))
    neg_dist = jnp.where(ok, jnp.where(is_meta, 0, -dist).astype(F32), NEG)
    rgrp = lax.broadcasted_iota(jnp.int32, (rows, 1), 0) // blk
    for h in range(ATTN_KV_HEADS):
        sl = slice(h * HEAD_DIM, (h + 1) * HEAD_DIM)
        kb = jnp.concatenate([km_ref[:, sl], kp_ref[:, sl], kc_ref[:, sl]], axis=0).astype(BF16)
        vb = jnp.concatenate([vm_ref[:, sl], vp_ref[:, sl], vc_ref[:, sl]], axis=0).astype(BF16)
        q3 = (jnp.concatenate(
            [q_ref[:, (h * g + i) * HEAD_DIM:(h * g + i + 1) * HEAD_DIM] for i in range(g)],
            axis=0).astype(F32) * scale).astype(BF16)
        slope = jnp.full((rows, 1), slopes[h * g], F32)
        sink = jnp.full((rows, 1), sink_ref[h * g], F32)
        for i in range(1, g):
            slope = jnp.where(rgrp == i, slopes[h * g + i], slope)
            sink = jnp.where(rgrp == i, sink_ref[h * g + i], sink)
        s = _dot_nt(q3, kb) + slope * neg_dist
        mx = jnp.maximum(jnp.max(s, axis=-1, keepdims=True), sink)
        e = jnp.exp(s - mx)
        den = jnp.sum(e, axis=-1, keepdims=True) + jnp.exp(sink - mx)
        o = _dot(e.astype(BF16), vb) / den
        for i in range(g):
            o_ref[:, (h * g + i) * HEAD_DIM:(h * g + i + 1) * HEAD_DIM] = (
                o[i * blk:(i + 1) * blk, :].astype(o_ref.dtype))


def swa_mixer(p, sinks, batch, tp, pad, col_q, col_k, col_v):
    nb = tp // ROW_BLOCK
    qw = ATTN_HEADS * HEAD_DIM
    kw = ATTN_KV_HEADS * HEAD_DIM
    assert col_q % qw == 0 and col_k % kw == 0 and col_v % kw == 0 and pad + N_META == ROW_BLOCK
    qc, kc, vc = col_q // qw, col_k // kw, col_v // kw
    cur = lambda c: (lambda b, j, s: (b * nb + j, c))
    prev = lambda c: (lambda b, j, s: (b * nb + jnp.maximum(j - 1, 0), c))
    meta = lambda c: (lambda b, j, s: (b * nb, c))
    grid_spec = pltpu.PrefetchScalarGridSpec(
        num_scalar_prefetch=1,
        grid=(batch, nb),
        in_specs=[pl.BlockSpec((ROW_BLOCK, qw), cur(qc)),
                  pl.BlockSpec((ROW_BLOCK, kw), prev(kc)),
                  pl.BlockSpec((ROW_BLOCK, kw), cur(kc)),
                  pl.BlockSpec((ROW_BLOCK, kw), meta(kc)),
                  pl.BlockSpec((ROW_BLOCK, kw), prev(vc)),
                  pl.BlockSpec((ROW_BLOCK, kw), cur(vc)),
                  pl.BlockSpec((ROW_BLOCK, kw), meta(vc))],
        out_specs=pl.BlockSpec((ROW_BLOCK, qw), lambda b, j, s: (b * nb + j, 0)),
    )
    return pl.pallas_call(
        functools.partial(_swa_kernel, slopes=_alibi_slopes(ATTN_HEADS), pad=pad),
        grid_spec=grid_spec,
        out_shape=jax.ShapeDtypeStruct((batch * tp, qw), BF16),
        compiler_params=_cparams(("parallel", "arbitrary")),
    )(sinks.astype(F32), p, p, p, p, p, p, p)


def _split3(x):
    p1 = x.astype(BF16)
    r1 = x - p1.astype(F32)
    p2 = r1.astype(BF16)
    p3 = (r1 - p2.astype(F32)).astype(BF16)
    return p1, p2, p3


def _rwkv_kernel(*refs, pad, nblk):
    pr_refs = refs[0:nblk]
    pk_refs = refs[nblk:2 * nblk]
    pv_refs = refs[2 * nblk:3 * nblk]
    (pl_ref, par_ref, mul_ref, w2_ref, a2_ref, g2_ref, seg_ref, o_ref,
     s_ref, cr_ref, ck_ref, cv_ref, cl_ref) = refs[3 * nblk:]
    tb, width = o_ref.shape
    cw = pr_refs[0].shape[1]
    npair = width // LANES
    c = CHUNK
    nch = tb // c
    first = pl.program_id(1) == 0

    @pl.when(first)
    def _():
        s_ref[...] = jnp.zeros_like(s_ref)
        cr_ref[...] = jnp.zeros_like(cr_ref)
        ck_ref[...] = jnp.zeros_like(ck_ref)
        cv_ref[...] = jnp.zeros_like(cv_ref)
        cl_ref[...] = jnp.zeros_like(cl_ref)

    row1 = lax.broadcasted_iota(jnp.int32, (tb, 1), 0)
    valid = (row1 >= pad) | jnp.logical_not(first)

    def shift_mix(x, c_ref, lanes, mu):
        x = jnp.where(valid, x.astype(F32), 0.0)
        prev = jnp.where(row1 == 0, c_ref[0:1, lanes], pltpu.roll(x, 1, axis=0))
        c_ref[0:1, lanes] = x[tb - 1:tb, :]
        return x + (prev - x) * mu

    lo = shift_mix(pl_ref[...], cl_ref, slice(None), mul_ref[...])
    wl = lo[:, 0:DECAY_RANK]
    al = lo[:, DECAY_RANK:DECAY_RANK + ICLR_RANK]
    gl = lo[:, DECAY_RANK + ICLR_RANK:DECAY_RANK + ICLR_RANK + g2_ref.shape[0]]
    z = -(par_ref[3:4, :] + _dot(jnp.tanh(wl).astype(BF16), w2_ref[...]))
    w = -(jnp.maximum(z, 0.0) + jnp.log(1.0 + jnp.exp(-jnp.abs(z)))) - 0.5
    lw_all = -jnp.exp(w)
    a_logit = par_ref[4:5, :] + _dot(al.astype(BF16), a2_ref[...])
    gate_all = _dot(_sigmoid(gl).astype(BF16), g2_ref[...])

    ti = lax.broadcasted_iota(jnp.int32, (tb, tb), 0)
    tj = lax.broadcasted_iota(jnp.int32, (tb, tb), 1)
    tri = jnp.where((ti // c == tj // c) & (tj <= ti), 1.0, 0.0).astype(BF16)
    cum_all = _dot(jnp.concatenate([tri, tri, tri], axis=1),
                   jnp.concatenate(_split3(lw_all), axis=0))

    seg = seg_ref[...]

    def seg_sum(x, two_pass=False):
        hi = x.astype(BF16)
        if two_pass:
            lo_ = (x - hi.astype(F32)).astype(BF16)
            return _dot(jnp.concatenate([hi, lo_], axis=1), seg)
        return _dot(hi, seg[0:LANES, :])

    r_hat, a_hat, b_til, k_til, v_p, cum_p, bonus_p = {}, {}, {}, {}, {}, {}, {}
    for pr in range(npair):
        ls = slice(pr * LANES, (pr + 1) * LANES)
        blk, off = divmod(pr * LANES, cw)
        src = slice(off, off + LANES)
        par = par_ref[:, ls]
        mu_r, mu_k, mu_v = par[0:1], par[1:2], par[2:3]
        k_k, k_a, r_k = par[5:6], par[6:7], par[7:8]
        r = shift_mix(pr_refs[blk][:, src], cr_ref, ls, mu_r)
        k = shift_mix(pk_refs[blk][:, src], ck_ref, ls, mu_k)
        v = shift_mix(pv_refs[blk][:, src], cv_ref, ls, mu_v)
        a = _sigmoid(a_logit[:, ls])
        kk = k * k_k
        kk = kk * lax.rsqrt(jnp.maximum(seg_sum(kk * kk, two_pass=True), 1e-24))
        k2 = k * (1.0 + (a - 1.0) * k_a)
        bonus_p[pr] = seg_sum(r * k2 * r_k) * v
        cum = cum_all[:, ls]
        dinv = jnp.exp(-cum)
        r_hat[pr] = r * jnp.exp(cum)
        a_hat[pr] = -kk * jnp.exp(cum - lw_all[:, ls])
        b_til[pr] = (kk * a) * dinv
        k_til[pr] = k2 * dinv
        v_p[pr] = v
        cum_p[pr] = cum

    ri = lax.broadcasted_iota(jnp.int32, (2 * c, 2 * c), 0)
    ci = lax.broadcasted_iota(jnp.int32, (2 * c, 2 * c), 1)
    head_mask = jnp.where((ri // c) == (ci // c), 1.0, 0.0).astype(BF16)

    def dup(xb):
        return jnp.concatenate([xb, xb], axis=0) * head_mask

    rt = lax.broadcasted_iota(jnp.int32, (c, 2 * c), 0)
    cs = lax.broadcasted_iota(jnp.int32, (c, 2 * c), 1) % c
    strict = cs < rt
    incl = cs <= rt
    eye = jnp.where(cs == rt, 1.0, 0.0)

    units = [(ch, pr) for ch in range(nch) for pr in range(npair)]

    a_s, gr, v_s, bk, d_tot, sc = {}, {}, {}, {}, {}, {}
    for u in units:
        ch, pr = u
        rs = slice(ch * c, (ch + 1) * c)
        last = rs.stop - 1
        d_tot[u] = jnp.exp(cum_p[pr][last:last + 1, :])
        a_u = a_hat[pr][rs, :].astype(BF16)
        gr[u] = r_hat[pr][rs, :].astype(BF16)
        a_s[u] = dup(a_u)
        v_s[u] = dup(v_p[pr][rs, :].astype(BF16))
        bt = b_til[pr][rs, :]
        kt = k_til[pr][rs, :]
        bk[u] = jnp.concatenate([dup((bt * d_tot[u]).astype(BF16)),
                                 dup((kt * d_tot[u]).astype(BF16))], axis=0)
        rhs = jnp.concatenate([dup(bt.astype(BF16)), dup(kt.astype(BF16))], axis=0)
        sc[u] = _dot_nt(jnp.concatenate([a_u, gr[u]], axis=0), rhs)

    lp, tinv, m_r, mak_v = {}, {}, {}, {}
    for u in units:
        l_ab = jnp.where(strict, sc[u][0:c, 0:2 * c], 0.0)
        m_ak = jnp.where(strict, sc[u][0:c, 2 * c:4 * c], 0.0).astype(BF16)
        m_r[u] = jnp.concatenate([jnp.where(incl, sc[u][c:2 * c, 0:2 * c], 0.0),
                                  jnp.where(incl, sc[u][c:2 * c, 2 * c:4 * c], 0.0)],
                                 axis=1).astype(BF16)
        lp[u] = l_ab.astype(BF16)
        tinv[u] = eye + l_ab
        mak_v[u] = _dot(m_ak, v_s[u]).astype(BF16)
    del sc

    nlev = int(math.log2(c))
    for u in units:
        lp[u] = _dot(lp[u], dup(lp[u])).astype(BF16)
    for lev in range(1, nlev - 1):
        for u in units:
            res = _dot(lp[u], jnp.concatenate([dup(lp[u]), dup(tinv[u].astype(BF16))], axis=1))
            tinv[u] = tinv[u] + res[:, 2 * c:4 * c]
            lp[u] = res[:, 0:2 * c].astype(BF16)
    for u in units:
        tinv[u] = tinv[u] + _dot(lp[u], dup(tinv[u].astype(BF16)))

    u_loc = {}
    for u in units:
        gu = _dot(tinv[u].astype(BF16), jnp.concatenate([a_s[u], dup(mak_v[u])], axis=1))
        gr[u] = jnp.concatenate([gu[:, 0:LANES].astype(BF16), gr[u]], axis=0)
        u_loc[u] = gu[:, LANES:2 * LANES]

    y_u = {}
    for ch in range(nch):
        xs = {}
        for pr in range(npair):
            u = (ch, pr)
            state = s_ref[pr]
            xs[u] = _dot_nt(gr[u], state.astype(BF16))
        for pr in range(npair):
            u = (ch, pr)
            us = (xs[u][0:c] + u_loc[u]).astype(BF16)
            uv = jnp.concatenate([dup(us), v_s[u]], axis=0)
            y_u[u] = xs[u][c:2 * c] + _dot(m_r[u], uv)
            s_ref[pr] = s_ref[pr] * d_tot[u] + _dot_tn(uv, bk[u])

    inv_n = 1.0 / RWKV_N
    for pr in range(npair):
        ls = slice(pr * LANES, (pr + 1) * LANES)
        y = jnp.concatenate([y_u[(ch, pr)] for ch in range(nch)], axis=0)
        mean = seg_sum(y) * inv_n
        yc = y - mean
        var = seg_sum(yc * yc) * inv_n
        yn = yc * lax.rsqrt(var + GN_EPS) * par_ref[8:9, ls] + par_ref[9:10, ls]
        o_ref[:, ls] = ((yn + bonus_p[pr]) * gate_all[:, ls]).astype(o_ref.dtype)


def rwkv_mixer(p, p_lora, par, mu_lora, w2, a2, g2, batch, tp, pad, col0, width):
    nb = tp // ROW_BLOCK
    lw = mu_lora.shape[1]
    cw = RWKV_COLS
    assert col0 % cw == 0 and width % cw == 0 and pad < ROW_BLOCK and p_lora.shape[1] == lw
    nblk = width // cw
    c0 = col0 // cw
    seg = (np.arange(LANES)[:, None] // RWKV_N == np.arange(LANES)[None, :] // RWKV_N)
    seg = jnp.asarray(np.concatenate([seg, seg], axis=0), BF16)
    const = lambda b, t: (0, 0)

    def col_block(cidx):
        return pl.BlockSpec((ROW_BLOCK, cw), lambda b, t: (b * nb + t, cidx))

    return pl.pallas_call(
        functools.partial(_rwkv_kernel, pad=pad, nblk=nblk),
        grid=(batch, nb),
        in_specs=[col_block(c0 + i) for i in range(3 * nblk)] + [
            pl.BlockSpec((ROW_BLOCK, lw), lambda b, t: (b * nb + t, 0)),
            pl.BlockSpec((16, width), const),
            pl.BlockSpec((1, lw), const),
            pl.BlockSpec((DECAY_RANK, width), const),
            pl.BlockSpec((ICLR_RANK, width), const),
            pl.BlockSpec((g2.shape[0], width), const),
            pl.BlockSpec((2 * LANES, LANES), const)],
        out_specs=pl.BlockSpec((ROW_BLOCK, width), lambda b, t: (b * nb + t, 0)),
        out_shape=jax.ShapeDtypeStruct((batch * tp, width), BF16),
        scratch_shapes=[pltpu.VMEM((width // LANES, LANES, LANES), F32),
                        pltpu.VMEM((8, width), F32), pltpu.VMEM((8, width), F32),
                        pltpu.VMEM((8, width), F32), pltpu.VMEM((8, lw), F32)],
        compiler_params=_cparams(("parallel", "arbitrary")),
    )(*([p] * (3 * nblk)), p_lora, par, mu_lora, w2, a2, g2, seg)


def _largest_tile(m, candidates):
    for t in candidates:
        if m % t == 0:
            return t
    return ROW_BLOCK


class _Tiles(NamedTuple):
    in_proj: tuple
    lora: tuple
    mix_out: tuple
    swiglu: tuple
    ffn_out: tuple
    conv_rows: int


def _choose_tiles(m, tp):
    return _Tiles(
        in_proj=(_largest_tile(m, (1280,)), 1024),
        lora=(_largest_tile(m, (1280,)), 512),
        mix_out=(_largest_tile(m, (1664, 1280)), 512),
        swiglu=(_largest_tile(m, (2080, 1664, 1280)), 256),
        ffn_out=(_largest_tile(m, (640,)), 512),
        conv_rows=_largest_tile(tp, (640,)))


def kernel(x, meta_tokens, norm_mix, w_in, conv_w, attn_sinks, rwkv_mu, rwkv_w0, rwkv_w2, rwkv_a0, rwkv_a2, rwkv_g2, rwkv_k_k, rwkv_k_a, rwkv_r_k, rwkv_ln_w, rwkv_ln_b, w_out, norm_ffn, w_ffn_in, w_ffn_out, norm_final):
    batch, seq, d = x.shape
    depth = w_in.shape[0]
    conv_width = conv_w.shape[1]
    attn_width = ATTN_HEADS * HEAD_DIM
    kv_width = ATTN_KV_HEADS * HEAD_DIM
    rwkv_width = rwkv_w0.shape[1]
    ffn_hidden = w_ffn_out.shape[1]
    assert seq % ROW_BLOCK == 0
    pad = ROW_BLOCK - N_META
    tp = seq + ROW_BLOCK
    nb = tp // ROW_BLOCK
    m = batch * tp

    lora_w = LORA_BLOCK
    gate_pad = -(-GATE_RANK // LANES) * LANES
    col_conv = 0
    col_q = CONV_K * conv_width
    col_k = col_q + attn_width
    col_v = col_k + kv_width
    col_rwkv = col_v + kv_width
    col_lora = col_rwkv + 3 * rwkv_width

    tiles = _choose_tiles(m, tp)

    w_main = (jnp.swapaxes(w_in, 1, 2) * norm_mix[:, None, :]).astype(BF16)
    w_lora = jnp.pad(w_main[:, col_lora:, :],
                     ((0, 0), (0, lora_w - (w_in.shape[2] - col_lora)), (0, 0)))
    w_out_b = w_out.astype(BF16)
    w_ffn_in_b = (w_ffn_in * norm_ffn[:, :, None]).astype(BF16)
    w_ffn_out_b = w_ffn_out.astype(BF16)

    h, hb, ssq = embed(x.reshape(batch * seq, d), meta_tokens.astype(x.dtype), batch, nb)
    for l in range(depth):
        mu = rwkv_mu[l]
        par = jnp.zeros((16, rwkv_width), F32)
        par = par.at[0].set(mu[0:rwkv_width])
        par = par.at[1].set(mu[rwkv_width:2 * rwkv_width])
        par = par.at[2].set(mu[2 * rwkv_width:3 * rwkv_width])
        par = par.at[3].set(rwkv_w0[l]).at[4].set(rwkv_a0[l]).at[5].set(rwkv_k_k[l])
        par = par.at[6].set(rwkv_k_a[l]).at[7].set(rwkv_r_k[l])
        par = par.at[8].set(rwkv_ln_w[l]).at[9].set(rwkv_ln_b[l])
        mu_lora = jnp.pad(mu[3 * rwkv_width:].reshape(1, -1),
                          ((0, 0), (0, lora_w - (mu.shape[0] - 3 * rwkv_width))))
        g2 = jnp.pad(rwkv_g2[l], ((0, gate_pad - rwkv_g2.shape[1]), (0, 0))).astype(BF16)

        p = normed_matmul(hb, ssq, w_main, l, col_lora, BF16, *tiles.in_proj)
        p_lora = normed_matmul(hb, ssq, w_lora, l, lora_w, F32, *tiles.lora)
        y_conv = conv_mixer(p, conv_w[l], batch, tp, pad, col_conv, conv_width, tiles.conv_rows)
        y_attn = swa_mixer(p, attn_sinks[l], batch, tp, pad, col_q, col_k, col_v)
        y_rwkv = rwkv_mixer(p, p_lora, par, mu_lora, rwkv_w2[l].astype(BF16),
                            rwkv_a2[l].astype(BF16), g2, batch, tp, pad, col_rwkv, rwkv_width)
        h, hb, ssq = _residual_call(_mix_out_kernel, [y_conv, y_attn, y_rwkv], w_out_b, l, h,
                                    *tiles.mix_out, vmem_limit=VMEM_LIMIT_HIGH)
        act = normed_swiglu(hb, ssq, w_ffn_in_b, l, ffn_hidden, *tiles.swiglu)
        h, hb, ssq = _residual_call(_mm_res_kernel, [act], w_ffn_out_b, l, h, *tiles.ffn_out,
                                    vmem_limit=VMEM_LIMIT_HIGH)

    out = final_norm(h, norm_final, batch, nb, x.dtype)
    return out.reshape(batch, seq, d)
```

```python
import functools
import math
from typing import NamedTuple

import numpy as np
import jax
import jax.numpy as jnp
from jax import lax
from jax.experimental import pallas as pl
from jax.experimental.pallas import tpu as pltpu

F32 = jnp.float32
BF16 = jnp.bfloat16

N_META = 16
NORM_EPS = 1e-5
GN_EPS = 64e-5
CONV_K = 3
HEAD_DIM = 128
ATTN_HEADS = 12
ATTN_KV_HEADS = 4
ATTN_GROUP = ATTN_HEADS // ATTN_KV_HEADS
WINDOW = 128
RWKV_N = 64
DECAY_RANK = 128
ICLR_RANK = 128
GATE_RANK = 480
LORA_BLOCK = 1024

LANES = 128
ROW_BLOCK = 128
CHUNK = 64
RWKV_COLS = 512
ROW_CHUNK = (416, 320)
VMEM_LIMIT = 56 * 1024 * 1024
VMEM_LIMIT_HIGH = 62 * 1024 * 1024
NEG = -1e30


def _alibi_slopes(n):
    def pow2_slopes(m):
        start = 2.0 ** (-8.0 / m)
        return [start ** (i + 1) for i in range(m)]
    c = 2 ** int(math.floor(math.log2(n)))
    s = pow2_slopes(c)
    if c < n:
        s = s + pow2_slopes(2 * c)[0::2][: n - c]
    return [float(np.float32(v)) for v in s]


def _cparams(sem):
    return pltpu.CompilerParams(dimension_semantics=sem, vmem_limit_bytes=VMEM_LIMIT)


def _dot(a, b):
    return jnp.dot(a, b, preferred_element_type=F32)


def _dot_nt(a, b):
    return lax.dot_general(a, b, (((1,), (1,)), ((), ())), preferred_element_type=F32)


def _dot_tn(a, b):
    return lax.dot_general(a, b, (((0,), (0,)), ((), ())), preferred_element_type=F32)


def _sigmoid(x):
    return 1.0 / (1.0 + jnp.exp(-x))


def _lane_partial_sumsq(x):
    sq = x * x
    acc = sq[:, 0:LANES]
    for c in range(1, x.shape[1] // LANES):
        acc = acc + sq[:, c * LANES:(c + 1) * LANES]
    return acc


def _row_scale(ssq_ref, d):
    return lax.rsqrt(jnp.sum(ssq_ref[...], axis=-1, keepdims=True) * (1.0 / d) + NORM_EPS)


def _embed_kernel(x_ref, meta_ref, h_ref, hb_ref, ssq_ref):
    pad = h_ref.shape[0] - meta_ref.shape[0]

    @pl.when(pl.program_id(1) == 0)
    def _():
        h_ref[0:pad, :] = jnp.zeros((pad, h_ref.shape[1]), h_ref.dtype)
        h_ref[pad:, :] = meta_ref[...]

    @pl.when(pl.program_id(1) > 0)
    def _():
        h_ref[...] = x_ref[...]

    h = h_ref[...]
    hb_ref[...] = h.astype(hb_ref.dtype)
    ssq_ref[...] = _lane_partial_sumsq(h)


def embed(x2, meta_tokens, batch, nb):
    d = x2.shape[1]
    nx = nb - 1
    m = batch * nb * ROW_BLOCK
    blk = pl.BlockSpec((ROW_BLOCK, d), lambda b, i: (b * nb + i, 0))
    return pl.pallas_call(
        _embed_kernel,
        grid=(batch, nb),
        in_specs=[pl.BlockSpec((ROW_BLOCK, d), lambda b, i: (b * nx + jnp.maximum(i - 1, 0), 0)),
                  pl.BlockSpec((N_META, d), lambda b, i: (0, 0))],
        out_specs=[blk, blk, pl.BlockSpec((ROW_BLOCK, LANES), lambda b, i: (b * nb + i, 0))],
        out_shape=[jax.ShapeDtypeStruct((m, d), F32),
                   jax.ShapeDtypeStruct((m, d), BF16),
                   jax.ShapeDtypeStruct((m, LANES), F32)],
        compiler_params=_cparams(("parallel", "arbitrary")),
    )(x2, meta_tokens)


def _final_norm_kernel(h_ref, g_ref, o_ref):
    x = h_ref[...]
    ms = jnp.mean(x * x, axis=-1, keepdims=True)
    o_ref[...] = (x * lax.rsqrt(ms + NORM_EPS) * g_ref[...]).astype(o_ref.dtype)


def final_norm(h, g, batch, nb, out_dtype):
    d = h.shape[1]
    nx = nb - 1
    return pl.pallas_call(
        _final_norm_kernel,
        grid=(batch, nx),
        in_specs=[pl.BlockSpec((ROW_BLOCK, d), lambda b, i: (b * nb + i + 1, 0)),
                  pl.BlockSpec((1, d), lambda b, i: (0, 0))],
        out_specs=pl.BlockSpec((ROW_BLOCK, d), lambda b, i: (b * nx + i, 0)),
        out_shape=jax.ShapeDtypeStruct((batch * nx * ROW_BLOCK, d), out_dtype),
        compiler_params=_cparams(("parallel", "parallel")),
    )(h, g.reshape(1, d))


def _normed_mm_kernel(a_ref, ssq_ref, wt_ref, o_ref):
    d = a_ref.shape[1]
    for rows in _row_chunks(o_ref.shape[0]):
        s = lax.rsqrt(jnp.sum(ssq_ref[rows, :], axis=-1, keepdims=True) * (1.0 / d) + NORM_EPS)
        o_ref[rows, :] = (_dot_nt(a_ref[rows, :], wt_ref[...]) * s).astype(o_ref.dtype)


def normed_matmul(hb, ssq, wt, layer, n, out_dtype, tm, tn, vmem_limit=VMEM_LIMIT):
    m, k = hb.shape
    tn = min(tn, n)
    w_spec = pl.BlockSpec((None, tn, k), lambda i, j: (layer, j, 0))
    return pl.pallas_call(
        _normed_mm_kernel,
        grid=(m // tm, n // tn),
        in_specs=[pl.BlockSpec((tm, k), lambda i, j: (i, 0)),
                  pl.BlockSpec((tm, LANES), lambda i, j: (i, 0)),
                  w_spec],
        out_specs=pl.BlockSpec((tm, tn), lambda i, j: (i, j)),
        out_shape=jax.ShapeDtypeStruct((m, n), out_dtype),
        compiler_params=pltpu.CompilerParams(dimension_semantics=("parallel", "arbitrary"),
                                             vmem_limit_bytes=vmem_limit),
    )(hb, ssq, wt)


def _normed_swiglu_kernel(a_ref, ssq_ref, wg_ref, wu_ref, o_ref):
    d = a_ref.shape[1]
    for rows in _row_chunks(o_ref.shape[0]):
        s = lax.rsqrt(jnp.sum(ssq_ref[rows, :], axis=-1, keepdims=True) * (1.0 / d) + NORM_EPS)
        a = a_ref[rows, :]
        g = _dot(a, wg_ref[...]) * s
        u = _dot(a, wu_ref[...]) * s
        o_ref[rows, :] = (g * _sigmoid(g) * u).astype(o_ref.dtype)


def normed_swiglu(hb, ssq, w, layer, hidden, tm, tn):
    m, k = hb.shape
    nt = hidden // tn
    assert hidden % tn == 0
    return pl.pallas_call(
        _normed_swiglu_kernel,
        grid=(m // tm, nt),
        in_specs=[pl.BlockSpec((tm, k), lambda i, j: (i, 0), pipeline_mode=pl.Buffered(1)),
                  pl.BlockSpec((tm, LANES), lambda i, j: (i, 0)),
                  pl.BlockSpec((None, k, tn), lambda i, j: (layer, 0, j)),
                  pl.BlockSpec((None, k, tn), lambda i, j: (layer, 0, j + nt))],
        out_specs=pl.BlockSpec((tm, tn), lambda i, j: (i, j)),
        out_shape=jax.ShapeDtypeStruct((m, hidden), BF16),
        compiler_params=_cparams(("parallel", "arbitrary")),
    )(hb, ssq, w, w)


def _row_chunks(tm):
    step = next((c for c in ROW_CHUNK if tm % c == 0), tm)
    return [slice(r, r + step) for r in range(0, tm, step)]


def _emit_residual(acc, rows, r_ref, h_ref, hb_ref, ssq_ref):
    h = r_ref[rows, :] + acc
    h_ref[rows, :] = h
    hb_ref[rows, :] = h.astype(hb_ref.dtype)
    ssq_ref[rows, :] += _lane_partial_sumsq(h)


def _zero_ssq_on_first_column_tile(ssq_ref):
    @pl.when(pl.program_id(1) == 0)
    def _():
        ssq_ref[...] = jnp.zeros_like(ssq_ref)


def _mix_out_kernel(a1_ref, a2_ref, a3_ref, w_ref, r_ref, h_ref, hb_ref, ssq_ref):
    k1, k2 = a1_ref.shape[1], a2_ref.shape[1]
    _zero_ssq_on_first_column_tile(ssq_ref)
    for rows in _row_chunks(h_ref.shape[0]):
        acc = _dot(a1_ref[rows, :], w_ref[0:k1, :])
        acc = acc + _dot(a2_ref[rows, :], w_ref[k1:k1 + k2, :])
        acc = acc + _dot(a3_ref[rows, :], w_ref[k1 + k2:, :])
        _emit_residual(acc, rows, r_ref, h_ref, hb_ref, ssq_ref)


def _mm_res_kernel(a_ref, w_ref, r_ref, h_ref, hb_ref, ssq_ref):
    _zero_ssq_on_first_column_tile(ssq_ref)
    for rows in _row_chunks(h_ref.shape[0]):
        _emit_residual(_dot(a_ref[rows, :], w_ref[...]), rows, r_ref, h_ref, hb_ref, ssq_ref)


def _residual_call(body, acts, w, layer, res, tm, tn, scratch=(), vmem_limit=VMEM_LIMIT):
    m, n = res.shape
    k = w.shape[1]
    tn = min(tn, n)
    tile = pl.BlockSpec((tm, tn), lambda i, j: (i, j))
    return pl.pallas_call(
        body,
        grid=(m // tm, n // tn),
        in_specs=[pl.BlockSpec((tm, a.shape[1]), lambda i, j: (i, 0)) for a in acts] + [
            pl.BlockSpec((None, k, tn), lambda i, j: (layer, 0, j)), tile],
        out_specs=[tile, tile, pl.BlockSpec((tm, LANES), lambda i, j: (i, 0))],
        out_shape=[jax.ShapeDtypeStruct((m, n), F32),
                   jax.ShapeDtypeStruct((m, n), BF16),
                   jax.ShapeDtypeStruct((m, LANES), F32)],
        scratch_shapes=list(scratch),
        compiler_params=pltpu.CompilerParams(dimension_semantics=("parallel", "arbitrary"),
                                             vmem_limit_bytes=vmem_limit),
    )(*acts, w, res)


def _conv_kernel(cb_ref, cc_ref, ch_ref, w_ref, o_ref, carry_ref, *, pad):
    first = pl.program_id(1) == 0

    @pl.when(first)
    def _():
        carry_ref[...] = jnp.zeros_like(carry_ref)

    u = cc_ref[...].astype(F32) * ch_ref[...].astype(F32)
    tt = u.shape[0]
    row = lax.broadcasted_iota(jnp.int32, u.shape, 0)
    u = jnp.where((row >= pad) | jnp.logical_not(first), u, 0.0)
    c0 = carry_ref[0:1, :]
    c1 = carry_ref[1:2, :]
    u1 = jnp.where(row == 0, c1, pltpu.roll(u, 1, axis=0))
    u2 = jnp.where(row == 0, c0, jnp.where(row == 1, c1, pltpu.roll(u, 2, axis=0)))
    carry_ref[0:2, :] = u[tt - 2:tt, :]
    y = u2 * w_ref[0:1, :]
    y = y + u1 * w_ref[1:2, :]
    y = y + u * w_ref[2:3, :]
    o_ref[...] = (cb_ref[...].astype(F32) * y).astype(o_ref.dtype)


def conv_mixer(p, conv_w, batch, tp, pad, col0, width, tt):
    nt = tp // tt
    assert pad <= tt and col0 % width == 0
    cb = col0 // width
    w = jnp.zeros((8, width), F32).at[0:CONV_K].set(conv_w.T)
    return pl.pallas_call(
        functools.partial(_conv_kernel, pad=pad),
        grid=(batch, nt),
        in_specs=[pl.BlockSpec((tt, width), lambda b, t: (b * nt + t, cb)),
                  pl.BlockSpec((tt, width), lambda b, t: (b * nt + t, cb + 1)),
                  pl.BlockSpec((tt, width), lambda b, t: (b * nt + t, cb + 2)),
                  pl.BlockSpec((8, width), lambda b, t: (0, 0))],
        out_specs=pl.BlockSpec((tt, width), lambda b, t: (b * nt + t, 0)),
        out_shape=jax.ShapeDtypeStruct((batch * tp, width), BF16),
        scratch_shapes=[pltpu.VMEM((8, width), F32)],
        compiler_params=_cparams(("parallel", "arbitrary")),
    )(p, p, p, w)


def _swa_kernel(sink_ref, q_ref, kp_ref, kc_ref, km_ref, vp_ref, vc_ref, vm_ref, o_ref, *,
                slopes, pad):
    j = pl.program_id(1)
    blk = ROW_BLOCK
    g = ATTN_GROUP
    scale = HEAD_DIM ** -0.5
    rows = g * blk
    ri = lax.broadcasted_iota(jnp.int32, (rows, 3 * blk), 0)
    ci = lax.broadcasted_iota(jnp.int32, (rows, 3 * blk), 1)
    q_pos = j * blk + ri % blk - pad
    k_pos = jnp.where(ci < blk, ci, (j - 2) * blk + ci) - pad
    dist = q_pos - k_pos
    is_meta = ci < blk
    ok = (is_meta & (k_pos >= 0) & (k_pos <= q_pos)) | (
        (ci >= blk) & (dist >= 0) & (dist < WINDOW) & (k_pos >= N_META))
    neg_dist = jnp.where(ok, jnp.where(is_meta, 0, -dist).astype(F32), NEG)
    rgrp = lax.broadcasted_iota(jnp.int32, (rows, 1), 0) // blk
    for h in range(ATTN_KV_HEADS):
        sl = slice(h * HEAD_DIM, (h + 1) * HEAD_DIM)
        kb = jnp.concatenate([km_ref[:, sl], kp_ref[:, sl], kc_ref[:, sl]], axis=0).astype(BF16)
        vb = jnp.concatenate([vm_ref[:, sl], vp_ref[:, sl], vc_ref[:, sl]], axis=0).astype(BF16)
        q3 = (jnp.concatenate(
            [q_ref[:, (h * g + i) * HEAD_DIM:(h * g + i + 1) * HEAD_DIM] for i in range(g)],
            axis=0).astype(F32) * scale).astype(BF16)
        slope = jnp.full((rows, 1), slopes[h * g], F32)
        sink = jnp.full((rows, 1), sink_ref[h * g], F32)
        for i in range(1, g):
            slope = jnp.where(rgrp == i, slopes[h * g + i], slope)
            sink = jnp.where(rgrp == i, sink_ref[h * g + i], sink)
        s = _dot_nt(q3, kb) + slope * neg_dist
        mx = jnp.maximum(jnp.max(s, axis=-1, keepdims=True), sink)
        e = jnp.exp(s - mx)
        den = jnp.sum(e, axis=-1, keepdims=True) + jnp.exp(sink - mx)
        o = _dot(e.astype(BF16), vb) / den
        for i in range(g):
            o_ref[:, (h * g + i) * HEAD_DIM:(h * g + i + 1) * HEAD_DIM] = (
                o[i * blk:(i + 1) * blk, :].astype(o_ref.dtype))


def swa_mixer(p, sinks, batch, tp, pad, col_q, col_k, col_v):
    nb = tp // ROW_BLOCK
    qw = ATTN_HEADS * HEAD_DIM
    kw = ATTN_KV_HEADS * HEAD_DIM
    assert col_q % qw == 0 and col_k % kw == 0 and col_v % kw == 0 and pad + N_META == ROW_BLOCK
    qc, kc, vc = col_q // qw, col_k // kw, col_v // kw
    cur = lambda c: (lambda b, j, s: (b * nb + j, c))
    prev = lambda c: (lambda b, j, s: (b * nb + jnp.maximum(j - 1, 0), c))
    meta = lambda c: (lambda b, j, s: (b * nb, c))
    grid_spec = pltpu.PrefetchScalarGridSpec(
        num_scalar_prefetch=1,
        grid=(batch, nb),
        in_specs=[pl.BlockSpec((ROW_BLOCK, qw), cur(qc)),
                  pl.BlockSpec((ROW_BLOCK, kw), prev(kc)),
                  pl.BlockSpec((ROW_BLOCK, kw), cur(kc)),
                  pl.BlockSpec((ROW_BLOCK, kw), meta(kc)),
                  pl.BlockSpec((ROW_BLOCK, kw), prev(vc)),
                  pl.BlockSpec((ROW_BLOCK, kw), cur(vc)),
                  pl.BlockSpec((ROW_BLOCK, kw), meta(vc))],
        out_specs=pl.BlockSpec((ROW_BLOCK, qw), lambda b, j, s: (b * nb + j, 0)),
    )
    return pl.pallas_call(
        functools.partial(_swa_kernel, slopes=_alibi_slopes(ATTN_HEADS), pad=pad),
        grid_spec=grid_spec,
        out_shape=jax.ShapeDtypeStruct((batch * tp, qw), BF16),
        compiler_params=_cparams(("parallel", "arbitrary")),
    )(sinks.astype(F32), p, p, p, p, p, p, p)


def _split3(x):
    p1 = x.astype(BF16)
    r1 = x - p1.astype(F32)
    p2 = r1.astype(BF16)
    p3 = (r1 - p2.astype(F32)).astype(BF16)
    return p1, p2, p3


def _rwkv_kernel(*refs, pad, nblk):
    pr_refs = refs[0:nblk]
    pk_refs = refs[nblk:2 * nblk]
    pv_refs = refs[2 * nblk:3 * nblk]
    (pl_ref, par_ref, mul_ref, w2_ref, a2_ref, g2_ref, seg_ref, o_ref,
     s_ref, cr_ref, ck_ref, cv_ref, cl_ref) = refs[3 * nblk:]
    tb, width = o_ref.shape
    cw = pr_refs[0].shape[1]
    npair = width // LANES
    c = CHUNK
    nch = tb // c
    first = pl.program_id(1) == 0

    @pl.when(first)
    def _():
        s_ref[...] = jnp.zeros_like(s_ref)
        cr_ref[...] = jnp.zeros_like(cr_ref)
        ck_ref[...] = jnp.zeros_like(ck_ref)
        cv_ref[...] = jnp.zeros_like(cv_ref)
        cl_ref[...] = jnp.zeros_like(cl_ref)

    row1 = lax.broadcasted_iota(jnp.int32, (tb, 1), 0)
    valid = (row1 >= pad) | jnp.logical_not(first)

    def shift_mix(x, c_ref, lanes, mu):
        x = jnp.where(valid, x.astype(F32), 0.0)
        prev = jnp.where(row1 == 0, c_ref[0:1, lanes], pltpu.roll(x, 1, axis=0))
        c_ref[0:1, lanes] = x[tb - 1:tb, :]
        return x + (prev - x) * mu

    lo = shift_mix(pl_ref[...], cl_ref, slice(None), mul_ref[...])
    wl = lo[:, 0:DECAY_RANK]
    al = lo[:, DECAY_RANK:DECAY_RANK + ICLR_RANK]
    gl = lo[:, DECAY_RANK + ICLR_RANK:DECAY_RANK + ICLR_RANK + g2_ref.shape[0]]
    z = -(par_ref[3:4, :] + _dot(jnp.tanh(wl).astype(BF16), w2_ref[...]))
    w = -(jnp.maximum(z, 0.0) + jnp.log(1.0 + jnp.exp(-jnp.abs(z)))) - 0.5
    lw_all = -jnp.exp(w)
    a_logit = par_ref[4:5, :] + _dot(al.astype(BF16), a2_ref[...])
    gate_all = _dot(_sigmoid(gl).astype(BF16), g2_ref[...])

    ti = lax.broadcasted_iota(jnp.int32, (tb, tb), 0)
    tj = lax.broadcasted_iota(jnp.int32, (tb, tb), 1)
    tri = jnp.where((ti // c == tj // c) & (tj <= ti), 1.0, 0.0).astype(BF16)
    cum_all = _dot(jnp.concatenate([tri, tri, tri], axis=1),
                   jnp.concatenate(_split3(lw_all), axis=0))

    seg = seg_ref[...]

    def seg_sum(x, two_pass=False):
        hi = x.astype(BF16)
        if two_pass:
            lo_ = (x - hi.astype(F32)).astype(BF16)
            return _dot(jnp.concatenate([hi, lo_], axis=1), seg)
        return _dot(hi, seg[0:LANES, :])

    r_hat, a_hat, b_til, k_til, v_p, cum_p, bonus_p = {}, {}, {}, {}, {}, {}, {}
    for pr in range(npair):
        ls = slice(pr * LANES, (pr + 1) * LANES)
        blk, off = divmod(pr * LANES, cw)
        src = slice(off, off + LANES)
        par = par_ref[:, ls]
        mu_r, mu_k, mu_v = par[0:1], par[1:2], par[2:3]
        k_k, k_a, r_k = par[5:6], par[6:7], par[7:8]
        r = shift_mix(pr_refs[blk][:, src], cr_ref, ls, mu_r)
        k = shift_mix(pk_refs[blk][:, src], ck_ref, ls, mu_k)
        v = shift_mix(pv_refs[blk][:, src], cv_ref, ls, mu_v)
        a = _sigmoid(a_logit[:, ls])
        kk = k * k_k
        kk = kk * lax.rsqrt(jnp.maximum(seg_sum(kk * kk, two_pass=True), 1e-24))
        k2 = k * (1.0 + (a - 1.0) * k_a)
        bonus_p[pr] = seg_sum(r * k2 * r_k) * v
        cum = cum_all[:, ls]
        dinv = jnp.exp(-cum)
        r_hat[pr] = r * jnp.exp(cum)
        a_hat[pr] = -kk * jnp.exp(cum - lw_all[:, ls])
        b_til[pr] = (kk * a) * dinv
        k_til[pr] = k2 * dinv
        v_p[pr] = v
        cum_p[pr] = cum

    ri = lax.broadcasted_iota(jnp.int32, (2 * c, 2 * c), 0)
    ci = lax.broadcasted_iota(jnp.int32, (2 * c, 2 * c), 1)
    head_mask = jnp.where((ri // c) == (ci // c), 1.0, 0.0).astype(BF16)

    def dup(xb):
        return jnp.concatenate([xb, xb], axis=0) * head_mask

    rt = lax.broadcasted_iota(jnp.int32, (c, 2 * c), 0)
    cs = lax.broadcasted_iota(jnp.int32, (c, 2 * c), 1) % c
    strict = cs < rt
    incl = cs <= rt
    eye = jnp.where(cs == rt, 1.0, 0.0)

    units = [(ch, pr) for ch in range(nch) for pr in range(npair)]

    a_s, gr, v_s, bk, d_tot, sc = {}, {}, {}, {}, {}, {}
    for u in units:
        ch, pr = u
        rs = slice(ch * c, (ch + 1) * c)
        last = rs.stop - 1
        d_tot[u] = jnp.exp(cum_p[pr][last:last + 1, :])
        a_u = a_hat[pr][rs, :].astype(BF16)
        gr[u] = r_hat[pr][rs, :].astype(BF16)
        a_s[u] = dup(a_u)
        v_s[u] = dup(v_p[pr][rs, :].astype(BF16))
        bt = b_til[pr][rs, :]
        kt = k_til[pr][rs, :]
        bk[u] = jnp.concatenate([dup((bt * d_tot[u]).astype(BF16)),
                                 dup((kt * d_tot[u]).astype(BF16))], axis=0)
        rhs = jnp.concatenate([dup(bt.astype(BF16)), dup(kt.astype(BF16))], axis=0)
        sc[u] = _dot_nt(jnp.concatenate([a_u, gr[u]], axis=0), rhs)

    lp, tinv, m_r, mak_v = {}, {}, {}, {}
    for u in units:
        l_ab = jnp.where(strict, sc[u][0:c, 0:2 * c], 0.0)
        m_ak = jnp.where(strict, sc[u][0:c, 2 * c:4 * c], 0.0).astype(BF16)
        m_r[u] = jnp.concatenate([jnp.where(incl, sc[u][c:2 * c, 0:2 * c], 0.0),
                                  jnp.where(incl, sc[u][c:2 * c, 2 * c:4 * c], 0.0)],
                                 axis=1).astype(BF16)
        lp[u] = l_ab.astype(BF16)
        tinv[u] = eye + l_ab
        mak_v[u] = _dot(m_ak, v_s[u]).astype(BF16)
    del sc

    nlev = int(math.log2(c))
    for u in units:
        lp[u] = _dot(lp[u], dup(lp[u])).astype(BF16)
    for lev in range(1, nlev - 1):
        for u in units:
            res = _dot(lp[u], jnp.concatenate([dup(lp[u]), dup(tinv[u].astype(BF16))], axis=1))
            tinv[u] = tinv[u] + res[:, 2 * c:4 * c]
            lp[u] = res[:, 0:2 * c].astype(BF16)
    for u in units:
        tinv[u] = tinv[u] + _dot(lp[u], dup(tinv[u].astype(BF16)))

    u_loc = {}
    for u in units:
        gu = _dot(tinv[u].astype(BF16), jnp.concatenate([a_s[u], dup(mak_v[u])], axis=1))
        gr[u] = jnp.concatenate([gu[:, 0:LANES].astype(BF16), gr[u]], axis=0)
        u_loc[u] = gu[:, LANES:2 * LANES]

    y_u = {}
    for ch in range(nch):
        xs = {}
        for pr in range(npair):
            u = (ch, pr)
            state = s_ref[pr]
            xs[u] = _dot_nt(gr[u], state.astype(BF16))
        for pr in range(npair):
            u = (ch, pr)
            us = (xs[u][0:c] + u_loc[u]).astype(BF16)
            uv = jnp.concatenate([dup(us), v_s[u]], axis=0)
            y_u[u] = xs[u][c:2 * c] + _dot(m_r[u], uv)
            s_ref[pr] = s_ref[pr] * d_tot[u] + _dot_tn(uv, bk[u])

    inv_n = 1.0 / RWKV_N
    for pr in range(npair):
        ls = slice(pr * LANES, (pr + 1) * LANES)
        y = jnp.concatenate([y_u[(ch, pr)] for ch in range(nch)], axis=0)
        mean = seg_sum(y) * inv_n
        yc = y - mean
        var = seg_sum(yc * yc) * inv_n
        yn = yc * lax.rsqrt(var + GN_EPS) * par_ref[8:9, ls] + par_ref[9:10, ls]
        o_ref[:, ls] = ((yn + bonus_p[pr]) * gate_all[:, ls]).astype(o_ref.dtype)


def rwkv_mixer(p, p_lora, par, mu_lora, w2, a2, g2, batch, tp, pad, col0, width):
    nb = tp // ROW_BLOCK
    lw = mu_lora.shape[1]
    cw = RWKV_COLS
    assert col0 % cw == 0 and width % cw == 0 and pad < ROW_BLOCK and p_lora.shape[1] == lw
    nblk = width // cw
    c0 = col0 // cw
    seg = (np.arange(LANES)[:, None] // RWKV_N == np.arange(LANES)[None, :] // RWKV_N)
    seg = jnp.asarray(np.concatenate([seg, seg], axis=0), BF16)
    const = lambda b, t: (0, 0)

    def col_block(cidx):
        return pl.BlockSpec((ROW_BLOCK, cw), lambda b, t: (b * nb + t, cidx))

    return pl.pallas_call(
        functools.partial(_rwkv_kernel, pad=pad, nblk=nblk),
        grid=(batch, nb),
        in_specs=[col_block(c0 + i) for i in range(3 * nblk)] + [
            pl.BlockSpec((ROW_BLOCK, lw), lambda b, t: (b * nb + t, 0)),
            pl.BlockSpec((16, width), const),
            pl.BlockSpec((1, lw), const),
            pl.BlockSpec((DECAY_RANK, width), const),
            pl.BlockSpec((ICLR_RANK, width), const),
            pl.BlockSpec((g2.shape[0], width), const),
            pl.BlockSpec((2 * LANES, LANES), const)],
        out_specs=pl.BlockSpec((ROW_BLOCK, width), lambda b, t: (b * nb + t, 0)),
        out_shape=jax.ShapeDtypeStruct((batch * tp, width), BF16),
        scratch_shapes=[pltpu.VMEM((width // LANES, LANES, LANES), F32),
                        pltpu.VMEM((8, width), F32), pltpu.VMEM((8, width), F32),
                        pltpu.VMEM((8, width), F32), pltpu.VMEM((8, lw), F32)],
        compiler_params=_cparams(("parallel", "arbitrary")),
    )(*([p] * (3 * nblk)), p_lora, par, mu_lora, w2, a2, g2, seg)


def _largest_tile(m, candidates):
    for t in candidates:
        if m % t == 0:
            return t
    return ROW_BLOCK


class _Tiles(NamedTuple):
    in_proj: tuple
    lora: tuple
    mix_out: tuple
    swiglu: tuple
    ffn_out: tuple
    conv_rows: int


def _choose_tiles(m, tp):
    return _Tiles(
        in_proj=(_largest_tile(m, (1664, 1280)), 1024),
        lora=(_largest_tile(m, (1280,)), 512),
        mix_out=(_largest_tile(m, (1664, 1280)), 512),
        swiglu=(_largest_tile(m, (3328, 2080, 1664, 1280)), 256),
        ffn_out=(_largest_tile(m, (640,)), 512),
        conv_rows=_largest_tile(tp, (640,)))


def kernel(x, meta_tokens, norm_mix, w_in, conv_w, attn_sinks, rwkv_mu, rwkv_w0, rwkv_w2, rwkv_a0, rwkv_a2, rwkv_g2, rwkv_k_k, rwkv_k_a, rwkv_r_k, rwkv_ln_w, rwkv_ln_b, w_out, norm_ffn, w_ffn_in, w_ffn_out, norm_final):
    batch, seq, d = x.shape
    depth = w_in.shape[0]
    conv_width = conv_w.shape[1]
    attn_width = ATTN_HEADS * HEAD_DIM
    kv_width = ATTN_KV_HEADS * HEAD_DIM
    rwkv_width = rwkv_w0.shape[1]
    ffn_hidden = w_ffn_out.shape[1]
    assert seq % ROW_BLOCK == 0
    pad = ROW_BLOCK - N_META
    tp = seq + ROW_BLOCK
    nb = tp // ROW_BLOCK
    m = batch * tp

    lora_w = LORA_BLOCK
    gate_pad = -(-GATE_RANK // LANES) * LANES
    col_conv = 0
    col_q = CONV_K * conv_width
    col_k = col_q + attn_width
    col_v = col_k + kv_width
    col_rwkv = col_v + kv_width
    col_lora = col_rwkv + 3 * rwkv_width

    tiles = _choose_tiles(m, tp)

    w_main = (jnp.swapaxes(w_in, 1, 2) * norm_mix[:, None, :]).astype(BF16)
    w_lora = jnp.pad(w_main[:, col_lora:, :],
                     ((0, 0), (0, lora_w - (w_in.shape[2] - col_lora)), (0, 0)))
    w_out_b = w_out.astype(BF16)
    w_ffn_in_b = (w_ffn_in * norm_ffn[:, :, None]).astype(BF16)
    w_ffn_out_b = w_ffn_out.astype(BF16)

    h, hb, ssq = embed(x.reshape(batch * seq, d), meta_tokens.astype(x.dtype), batch, nb)
    for l in range(depth):
        mu = rwkv_mu[l]
        par = jnp.zeros((16, rwkv_width), F32)
        par = par.at[0].set(mu[0:rwkv_width])
        par = par.at[1].set(mu[rwkv_width:2 * rwkv_width])
        par = par.at[2].set(mu[2 * rwkv_width:3 * rwkv_width])
        par = par.at[3].set(rwkv_w0[l]).at[4].set(rwkv_a0[l]).at[5].set(rwkv_k_k[l])
        par = par.at[6].set(rwkv_k_a[l]).at[7].set(rwkv_r_k[l])
        par = par.at[8].set(rwkv_ln_w[l]).at[9].set(rwkv_ln_b[l])
        mu_lora = jnp.pad(mu[3 * rwkv_width:].reshape(1, -1),
                          ((0, 0), (0, lora_w - (mu.shape[0] - 3 * rwkv_width))))
        g2 = jnp.pad(rwkv_g2[l], ((0, gate_pad - rwkv_g2.shape[1]), (0, 0))).astype(BF16)

        p = normed_matmul(hb, ssq, w_main, l, col_lora, BF16, *tiles.in_proj,
                          vmem_limit=VMEM_LIMIT_HIGH)
        p_lora = normed_matmul(hb, ssq, w_lora, l, lora_w, F32, *tiles.lora)
        y_conv = conv_mixer(p, conv_w[l], batch, tp, pad, col_conv, conv_width, tiles.conv_rows)
        y_attn = swa_mixer(p, attn_sinks[l], batch, tp, pad, col_q, col_k, col_v)
        y_rwkv = rwkv_mixer(p, p_lora, par, mu_lora, rwkv_w2[l].astype(BF16),
                            rwkv_a2[l].astype(BF16), g2, batch, tp, pad, col_rwkv, rwkv_width)
        h, hb, ssq = _residual_call(_mix_out_kernel, [y_conv, y_attn, y_rwkv], w_out_b, l, h,
                                    *tiles.mix_out, vmem_limit=VMEM_LIMIT_HIGH)
        act = normed_swiglu(hb, ssq, w_ffn_in_b, l, ffn_hidden, *tiles.swiglu)
        h, hb, ssq = _residual_call(_mm_res_kernel, [act], w_ffn_out_b, l, h, *tiles.ffn_out,
                                    vmem_limit=VMEM_LIMIT_HIGH)

    out = final_norm(h, norm_final, batch, nb, x.dtype)
    return out.reshape(batch, seq, d)
```

```python
import functools
import math
from typing import NamedTuple

import numpy as np
import jax
import jax.numpy as jnp
from jax import lax
from jax.experimental import pallas as pl
from jax.experimental.pallas import tpu as pltpu

F32 = jnp.float32
BF16 = jnp.bfloat16

N_META = 16
NORM_EPS = 1e-5
GN_EPS = 64e-5
CONV_K = 3
HEAD_DIM = 128
ATTN_HEADS = 12
ATTN_KV_HEADS = 4
ATTN_GROUP = ATTN_HEADS // ATTN_KV_HEADS
WINDOW = 128
RWKV_N = 64
DECAY_RANK = 128
ICLR_RANK = 128
GATE_RANK = 480
LORA_BLOCK = 1024

LANES = 128
ROW_BLOCK = 128
CHUNK = 64
RWKV_COLS = 512
ROW_CHUNK = (416, 320)
VMEM_LIMIT = 56 * 1024 * 1024
VMEM_LIMIT_HIGH = 62 * 1024 * 1024
NEG = -1e30


def _alibi_slopes(n):
    def pow2_slopes(m):
        start = 2.0 ** (-8.0 / m)
        return [start ** (i + 1) for i in range(m)]
    c = 2 ** int(math.floor(math.log2(n)))
    s = pow2_slopes(c)
    if c < n:
        s = s + pow2_slopes(2 * c)[0::2][: n - c]
    return [float(np.float32(v)) for v in s]


def _cparams(sem):
    return pltpu.CompilerParams(dimension_semantics=sem, vmem_limit_bytes=VMEM_LIMIT)


def _dot(a, b):
    return jnp.dot(a, b, preferred_element_type=F32)


def _dot_nt(a, b):
    return lax.dot_general(a, b, (((1,), (1,)), ((), ())), preferred_element_type=F32)


def _dot_tn(a, b):
    return lax.dot_general(a, b, (((0,), (0,)), ((), ())), preferred_element_type=F32)


def _sigmoid(x):
    return 1.0 / (1.0 + jnp.exp(-x))


def _lane_partial_sumsq(x):
    sq = x * x
    acc = sq[:, 0:LANES]
    for c in range(1, x.shape[1] // LANES):
        acc = acc + sq[:, c * LANES:(c + 1) * LANES]
    return acc


def _row_scale(ssq_ref, d):
    return lax.rsqrt(jnp.sum(ssq_ref[...], axis=-1, keepdims=True) * (1.0 / d) + NORM_EPS)


def _embed_kernel(x_ref, meta_ref, h_ref, hb_ref, ssq_ref):
    pad = h_ref.shape[0] - meta_ref.shape[0]

    @pl.when(pl.program_id(1) == 0)
    def _():
        h_ref[0:pad, :] = jnp.zeros((pad, h_ref.shape[1]), h_ref.dtype)
        h_ref[pad:, :] = meta_ref[...]

    @pl.when(pl.program_id(1) > 0)
    def _():
        h_ref[...] = x_ref[...]

    h = h_ref[...]
    hb_ref[...] = h.astype(hb_ref.dtype)
    ssq_ref[...] = _lane_partial_sumsq(h)


def embed(x2, meta_tokens, batch, nb):
    d = x2.shape[1]
    nx = nb - 1
    m = batch * nb * ROW_BLOCK
    blk = pl.BlockSpec((ROW_BLOCK, d), lambda b, i: (b * nb + i, 0))
    return pl.pallas_call(
        _embed_kernel,
        grid=(batch, nb),
        in_specs=[pl.BlockSpec((ROW_BLOCK, d), lambda b, i: (b * nx + jnp.maximum(i - 1, 0), 0)),
                  pl.BlockSpec((N_META, d), lambda b, i: (0, 0))],
        out_specs=[blk, blk, pl.BlockSpec((ROW_BLOCK, LANES), lambda b, i: (b * nb + i, 0))],
        out_shape=[jax.ShapeDtypeStruct((m, d), F32),
                   jax.ShapeDtypeStruct((m, d), BF16),
                   jax.ShapeDtypeStruct((m, LANES), F32)],
        compiler_params=_cparams(("parallel", "arbitrary")),
    )(x2, meta_tokens)


def _final_norm_kernel(h_ref, g_ref, o_ref):
    x = h_ref[...]
    ms = jnp.mean(x * x, axis=-1, keepdims=True)
    o_ref[...] = (x * lax.rsqrt(ms + NORM_EPS) * g_ref[...]).astype(o_ref.dtype)


def final_norm(h, g, batch, nb, out_dtype):
    d = h.shape[1]
    nx = nb - 1
    return pl.pallas_call(
        _final_norm_kernel,
        grid=(batch, nx),
        in_specs=[pl.BlockSpec((ROW_BLOCK, d), lambda b, i: (b * nb + i + 1, 0)),
                  pl.BlockSpec((1, d), lambda b, i: (0, 0))],
        out_specs=pl.BlockSpec((ROW_BLOCK, d), lambda b, i: (b * nx + i, 0)),
        out_shape=jax.ShapeDtypeStruct((batch * nx * ROW_BLOCK, d), out_dtype),
        compiler_params=_cparams(("parallel", "parallel")),
    )(h, g.reshape(1, d))


def _normed_mm_kernel(a_ref, ssq_ref, wt_ref, o_ref):
    d = a_ref.shape[1]
    for rows in _row_chunks(o_ref.shape[0]):
        s = lax.rsqrt(jnp.sum(ssq_ref[rows, :], axis=-1, keepdims=True) * (1.0 / d) + NORM_EPS)
        o_ref[rows, :] = (_dot_nt(a_ref[rows, :], wt_ref[...]) * s).astype(o_ref.dtype)


def normed_matmul(hb, ssq, wt, layer, n, out_dtype, tm, tn, vmem_limit=VMEM_LIMIT):
    m, k = hb.shape
    tn = min(tn, n)
    w_spec = pl.BlockSpec((None, tn, k), lambda i, j: (layer, j, 0))
    return pl.pallas_call(
        _normed_mm_kernel,
        grid=(m // tm, n // tn),
        in_specs=[pl.BlockSpec((tm, k), lambda i, j: (i, 0)),
                  pl.BlockSpec((tm, LANES), lambda i, j: (i, 0)),
                  w_spec],
        out_specs=pl.BlockSpec((tm, tn), lambda i, j: (i, j)),
        out_shape=jax.ShapeDtypeStruct((m, n), out_dtype),
        compiler_params=pltpu.CompilerParams(dimension_semantics=("parallel", "arbitrary"),
                                             vmem_limit_bytes=vmem_limit),
    )(hb, ssq, wt)


def _normed_swiglu_kernel(a_ref, ssq_ref, gain_ref, wg_ref, wu_ref, o_ref):
    d = a_ref.shape[1]
    gain = gain_ref[...]
    wg = (wg_ref[...] * gain).astype(BF16)
    wu = (wu_ref[...] * gain).astype(BF16)
    for rows in _row_chunks(o_ref.shape[0]):
        s = lax.rsqrt(jnp.sum(ssq_ref[rows, :], axis=-1, keepdims=True) * (1.0 / d) + NORM_EPS)
        a = a_ref[rows, :]
        g = _dot(a, wg) * s
        u = _dot(a, wu) * s
        o_ref[rows, :] = (g * _sigmoid(g) * u).astype(o_ref.dtype)


def normed_swiglu(hb, ssq, gain, w, layer, hidden, tm, tn):
    m, k = hb.shape
    nt = hidden // tn
    assert hidden % tn == 0
    return pl.pallas_call(
        _normed_swiglu_kernel,
        grid=(m // tm, nt),
        in_specs=[pl.BlockSpec((tm, k), lambda i, j: (i, 0), pipeline_mode=pl.Buffered(1)),
                  pl.BlockSpec((tm, LANES), lambda i, j: (i, 0)),
                  pl.BlockSpec((None, k, 1), lambda i, j: (layer, 0, 0)),
                  pl.BlockSpec((None, k, tn), lambda i, j: (layer, 0, j)),
                  pl.BlockSpec((None, k, tn), lambda i, j: (layer, 0, j + nt))],
        out_specs=pl.BlockSpec((tm, tn), lambda i, j: (i, j)),
        out_shape=jax.ShapeDtypeStruct((m, hidden), BF16),
        compiler_params=pltpu.CompilerParams(dimension_semantics=("parallel", "arbitrary"),
                                             vmem_limit_bytes=VMEM_LIMIT_HIGH),
    )(hb, ssq, gain, w, w)


def _row_chunks(tm):
    step = next((c for c in ROW_CHUNK if tm % c == 0), tm)
    return [slice(r, r + step) for r in range(0, tm, step)]


def _emit_residual(acc, rows, r_ref, h_ref, hb_ref, ssq_ref):
    h = r_ref[rows, :] + acc
    h_ref[rows, :] = h
    hb_ref[rows, :] = h.astype(hb_ref.dtype)
    ssq_ref[rows, :] += _lane_partial_sumsq(h)


def _zero_ssq_on_first_column_tile(ssq_ref):
    @pl.when(pl.program_id(1) == 0)
    def _():
        ssq_ref[...] = jnp.zeros_like(ssq_ref)


def _mix_out_kernel(a1_ref, a2_ref, a3_ref, w_ref, r_ref, h_ref, hb_ref, ssq_ref):
    k1, k2 = a1_ref.shape[1], a2_ref.shape[1]
    _zero_ssq_on_first_column_tile(ssq_ref)
    for rows in _row_chunks(h_ref.shape[0]):
        acc = _dot(a1_ref[rows, :], w_ref[0:k1, :])
        acc = acc + _dot(a2_ref[rows, :], w_ref[k1:k1 + k2, :])
        acc = acc + _dot(a3_ref[rows, :], w_ref[k1 + k2:, :])
        _emit_residual(acc, rows, r_ref, h_ref, hb_ref, ssq_ref)


def _mm_res_kernel(a_ref, w_ref, r_ref, h_ref, hb_ref, ssq_ref):
    _zero_ssq_on_first_column_tile(ssq_ref)
    for rows in _row_chunks(h_ref.shape[0]):
        _emit_residual(_dot(a_ref[rows, :], w_ref[...]), rows, r_ref, h_ref, hb_ref, ssq_ref)


def _residual_call(body, acts, w, layer, res, tm, tn, scratch=(), vmem_limit=VMEM_LIMIT):
    m, n = res.shape
    k = w.shape[1]
    tn = min(tn, n)
    tile = pl.BlockSpec((tm, tn), lambda i, j: (i, j))
    return pl.pallas_call(
        body,
        grid=(m // tm, n // tn),
        in_specs=[pl.BlockSpec((tm, a.shape[1]), lambda i, j: (i, 0)) for a in acts] + [
            pl.BlockSpec((None, k, tn), lambda i, j: (layer, 0, j)), tile],
        out_specs=[tile, tile, pl.BlockSpec((tm, LANES), lambda i, j: (i, 0))],
        out_shape=[jax.ShapeDtypeStruct((m, n), F32),
                   jax.ShapeDtypeStruct((m, n), BF16),
                   jax.ShapeDtypeStruct((m, LANES), F32)],
        scratch_shapes=list(scratch),
        compiler_params=pltpu.CompilerParams(dimension_semantics=("parallel", "arbitrary"),
                                             vmem_limit_bytes=vmem_limit),
    )(*acts, w, res)


def _conv_kernel(cb_ref, cc_ref, ch_ref, w_ref, o_ref, carry_ref, *, pad):
    first = pl.program_id(1) == 0

    @pl.when(first)
    def _():
        carry_ref[...] = jnp.zeros_like(carry_ref)

    u = cc_ref[...].astype(F32) * ch_ref[...].astype(F32)
    tt = u.shape[0]
    row = lax.broadcasted_iota(jnp.int32, u.shape, 0)
    u = jnp.where((row >= pad) | jnp.logical_not(first), u, 0.0)
    c0 = carry_ref[0:1, :]
    c1 = carry_ref[1:2, :]
    u1 = jnp.where(row == 0, c1, pltpu.roll(u, 1, axis=0))
    u2 = jnp.where(row == 0, c0, jnp.where(row == 1, c1, pltpu.roll(u, 2, axis=0)))
    carry_ref[0:2, :] = u[tt - 2:tt, :]
    y = u2 * w_ref[0:1, :]
    y = y + u1 * w_ref[1:2, :]
    y = y + u * w_ref[2:3, :]
    o_ref[...] = (cb_ref[...].astype(F32) * y).astype(o_ref.dtype)


def conv_mixer(p, conv_w, batch, tp, pad, col0, width, tt):
    nt = tp // tt
    assert pad <= tt and col0 % width == 0
    cb = col0 // width
    w = jnp.zeros((8, width), F32).at[0:CONV_K].set(conv_w.T)
    return pl.pallas_call(
        functools.partial(_conv_kernel, pad=pad),
        grid=(batch, nt),
        in_specs=[pl.BlockSpec((tt, width), lambda b, t: (b * nt + t, cb)),
                  pl.BlockSpec((tt, width), lambda b, t: (b * nt + t, cb + 1)),
                  pl.BlockSpec((tt, width), lambda b, t: (b * nt + t, cb + 2)),
                  pl.BlockSpec((8, width), lambda b, t: (0, 0))],
        out_specs=pl.BlockSpec((tt, width), lambda b, t: (b * nt + t, 0)),
        out_shape=jax.ShapeDtypeStruct((batch * tp, width), BF16),
        scratch_shapes=[pltpu.VMEM((8, width), F32)],
        compiler_params=_cparams(("parallel", "arbitrary")),
    )(p, p, p, w)


def _swa_kernel(sink_ref, q_ref, kp_ref, kc_ref, km_ref, vp_ref, vc_ref, vm_ref, o_ref, *,
                slopes, pad):
    j = pl.program_id(1)
    blk = ROW_BLOCK
    g = ATTN_GROUP
    scale = HEAD_DIM ** -0.5
    rows = g * blk
    ri = lax.broadcasted_iota(jnp.int32, (rows, 3 * blk), 0)
    ci = lax.broadcasted_iota(jnp.int32, (rows, 3 * blk), 1)
    q_pos = j * blk + ri % blk - pad
    k_pos = jnp.where(ci < blk, ci, (j - 2) * blk + ci) - pad
    dist = q_pos - k_pos
    is_meta = ci < blk
    ok = (is_meta & (k_pos >= 0) & (k_pos <= q_pos)) | (
        (ci >= blk) & (dist >= 0) & (dist < WINDOW) & (k_pos >= N_META))
    neg_dist = jnp.where(ok, jnp.where(is_meta, 0, -dist).astype(F32), NEG)
    rgrp = lax.broadcasted_iota(jnp.int32, (rows, 1), 0) // blk
    for h in range(ATTN_KV_HEADS):
        sl = slice(h * HEAD_DIM, (h + 1) * HEAD_DIM)
        kb = jnp.concatenate([km_ref[:, sl], kp_ref[:, sl], kc_ref[:, sl]], axis=0).astype(BF16)
        vb = jnp.concatenate([vm_ref[:, sl], vp_ref[:, sl], vc_ref[:, sl]], axis=0).astype(BF16)
        q3 = (jnp.concatenate(
            [q_ref[:, (h * g + i) * HEAD_DIM:(h * g + i + 1) * HEAD_DIM] for i in range(g)],
            axis=0).astype(F32) * scale).astype(BF16)
        slope = jnp.full((rows, 1), slopes[h * g], F32)
        sink = jnp.full((rows, 1), sink_ref[h * g], F32)
        for i in range(1, g):
            slope = jnp.where(rgrp == i, slopes[h * g + i], slope)
            sink = jnp.where(rgrp == i, sink_ref[h * g + i], sink)
        s = _dot_nt(q3, kb) + slope * neg_dist
        mx = jnp.maximum(jnp.max(s, axis=-1, keepdims=True), sink)
        e = jnp.exp(s - mx)
        den = jnp.sum(e, axis=-1, keepdims=True) + jnp.exp(sink - mx)
        o = _dot(e.astype(BF16), vb) / den
        for i in range(g):
            o_ref[:, (h * g + i) * HEAD_DIM:(h * g + i + 1) * HEAD_DIM] = (
                o[i * blk:(i + 1) * blk, :].astype(o_ref.dtype))


def swa_mixer(p, sinks, batch, tp, pad, col_q, col_k, col_v):
    nb = tp // ROW_BLOCK
    qw = ATTN_HEADS * HEAD_DIM
    kw = ATTN_KV_HEADS * HEAD_DIM
    assert col_q % qw == 0 and col_k % kw == 0 and col_v % kw == 0 and pad + N_META == ROW_BLOCK
    qc, kc, vc = col_q // qw, col_k // kw, col_v // kw
    cur = lambda c: (lambda b, j, s: (b * nb + j, c))
    prev = lambda c: (lambda b, j, s: (b * nb + jnp.maximum(j - 1, 0), c))
    meta = lambda c: (lambda b, j, s: (b * nb, c))
    grid_spec = pltpu.PrefetchScalarGridSpec(
        num_scalar_prefetch=1,
        grid=(batch, nb),
        in_specs=[pl.BlockSpec((ROW_BLOCK, qw), cur(qc)),
                  pl.BlockSpec((ROW_BLOCK, kw), prev(kc)),
                  pl.BlockSpec((ROW_BLOCK, kw), cur(kc)),
                  pl.BlockSpec((ROW_BLOCK, kw), meta(kc)),
                  pl.BlockSpec((ROW_BLOCK, kw), prev(vc)),
                  pl.BlockSpec((ROW_BLOCK, kw), cur(vc)),
                  pl.BlockSpec((ROW_BLOCK, kw), meta(vc))],
        out_specs=pl.BlockSpec((ROW_BLOCK, qw), lambda b, j, s: (b * nb + j, 0)),
    )
    return pl.pallas_call(
        functools.partial(_swa_kernel, slopes=_alibi_slopes(ATTN_HEADS), pad=pad),
        grid_spec=grid_spec,
        out_shape=jax.ShapeDtypeStruct((batch * tp, qw), BF16),
        compiler_params=_cparams(("parallel", "arbitrary")),
    )(sinks.astype(F32), p, p, p, p, p, p, p)


def _split3(x):
    p1 = x.astype(BF16)
    r1 = x - p1.astype(F32)
    p2 = r1.astype(BF16)
    p3 = (r1 - p2.astype(F32)).astype(BF16)
    return p1, p2, p3


def _rwkv_kernel(*refs, pad, nblk):
    pr_refs = refs[0:nblk]
    pk_refs = refs[nblk:2 * nblk]
    pv_refs = refs[2 * nblk:3 * nblk]
    (pl_ref, par_ref, mul_ref, w2_ref, a2_ref, g2_ref, seg_ref, o_ref,
     s_ref, cr_ref, ck_ref, cv_ref, cl_ref) = refs[3 * nblk:]
    tb, width = o_ref.shape
    cw = pr_refs[0].shape[1]
    npair = width // LANES
    c = CHUNK
    nch = tb // c
    first = pl.program_id(1) == 0

    @pl.when(first)
    def _():
        s_ref[...] = jnp.zeros_like(s_ref)
        cr_ref[...] = jnp.zeros_like(cr_ref)
        ck_ref[...] = jnp.zeros_like(ck_ref)
        cv_ref[...] = jnp.zeros_like(cv_ref)
        cl_ref[...] = jnp.zeros_like(cl_ref)

    row1 = lax.broadcasted_iota(jnp.int32, (tb, 1), 0)
    valid = (row1 >= pad) | jnp.logical_not(first)

    def shift_mix(x, c_ref, lanes, mu):
        x = jnp.where(valid, x.astype(F32), 0.0)
        prev = jnp.where(row1 == 0, c_ref[0:1, lanes], pltpu.roll(x, 1, axis=0))
        c_ref[0:1, lanes] = x[tb - 1:tb, :]
        return x + (prev - x) * mu

    lo = shift_mix(pl_ref[...], cl_ref, slice(None), mul_ref[...])
    wl = lo[:, 0:DECAY_RANK]
    al = lo[:, DECAY_RANK:DECAY_RANK + ICLR_RANK]
    gl = lo[:, DECAY_RANK + ICLR_RANK:DECAY_RANK + ICLR_RANK + g2_ref.shape[0]]
    z = -(par_ref[3:4, :] + _dot(jnp.tanh(wl).astype(BF16), w2_ref[...]))
    w = -(jnp.maximum(z, 0.0) + jnp.log(1.0 + jnp.exp(-jnp.abs(z)))) - 0.5
    lw_all = -jnp.exp(w)
    a_logit = par_ref[4:5, :] + _dot(al.astype(BF16), a2_ref[...])
    gate_all = _dot(_sigmoid(gl).astype(BF16), g2_ref[...])

    ti = lax.broadcasted_iota(jnp.int32, (tb, tb), 0)
    tj = lax.broadcasted_iota(jnp.int32, (tb, tb), 1)
    tri = jnp.where((ti // c == tj // c) & (tj <= ti), 1.0, 0.0).astype(BF16)
    cum_all = _dot(jnp.concatenate([tri, tri, tri], axis=1),
                   jnp.concatenate(_split3(lw_all), axis=0))

    seg = seg_ref[...]

    def seg_sum(x, two_pass=False):
        hi = x.astype(BF16)
        if two_pass:
            lo_ = (x - hi.astype(F32)).astype(BF16)
            return _dot(jnp.concatenate([hi, lo_], axis=1), seg)
        return _dot(hi, seg[0:LANES, :])

    r_hat, a_hat, b_til, k_til, v_p, cum_p, bonus_p = {}, {}, {}, {}, {}, {}, {}
    for pr in range(npair):
        ls = slice(pr * LANES, (pr + 1) * LANES)
        blk, off = divmod(pr * LANES, cw)
        src = slice(off, off + LANES)
        par = par_ref[:, ls]
        mu_r, mu_k, mu_v = par[0:1], par[1:2], par[2:3]
        k_k, k_a, r_k = par[5:6], par[6:7], par[7:8]
        r = shift_mix(pr_refs[blk][:, src], cr_ref, ls, mu_r)
        k = shift_mix(pk_refs[blk][:, src], ck_ref, ls, mu_k)
        v = shift_mix(pv_refs[blk][:, src], cv_ref, ls, mu_v)
        a = _sigmoid(a_logit[:, ls])
        kk = k * k_k
        kk = kk * lax.rsqrt(jnp.maximum(seg_sum(kk * kk, two_pass=True), 1e-24))
        k2 = k * (1.0 + (a - 1.0) * k_a)
        bonus_p[pr] = seg_sum(r * k2 * r_k) * v
        cum = cum_all[:, ls]
        dinv = jnp.exp(-cum)
        r_hat[pr] = r * jnp.exp(cum)
        a_hat[pr] = -kk * jnp.exp(cum - lw_all[:, ls])
        b_til[pr] = (kk * a) * dinv
        k_til[pr] = k2 * dinv
        v_p[pr] = v
        cum_p[pr] = cum

    ri = lax.broadcasted_iota(jnp.int32, (2 * c, 2 * c), 0)
    ci = lax.broadcasted_iota(jnp.int32, (2 * c, 2 * c), 1)
    head_mask = jnp.where((ri // c) == (ci // c), 1.0, 0.0).astype(BF16)

    def dup(xb):
        return jnp.concatenate([xb, xb], axis=0) * head_mask

    rt = lax.broadcasted_iota(jnp.int32, (c, 2 * c), 0)
    cs = lax.broadcasted_iota(jnp.int32, (c, 2 * c), 1) % c
    strict = cs < rt
    incl = cs <= rt
    eye = jnp.where(cs == rt, 1.0, 0.0)

    units = [(ch, pr) for ch in range(nch) for pr in range(npair)]

    a_s, gr, v_s, bk, d_tot, sc = {}, {}, {}, {}, {}, {}
    for u in units:
        ch, pr = u
        rs = slice(ch * c, (ch + 1) * c)
        last = rs.stop - 1
        d_tot[u] = jnp.exp(cum_p[pr][last:last + 1, :])
        a_u = a_hat[pr][rs, :].astype(BF16)
        gr[u] = r_hat[pr][rs, :].astype(BF16)
        a_s[u] = dup(a_u)
        v_s[u] = dup(v_p[pr][rs, :].astype(BF16))
        bt = b_til[pr][rs, :]
        kt = k_til[pr][rs, :]
        bk[u] = jnp.concatenate([dup((bt * d_tot[u]).astype(BF16)),
                                 dup((kt * d_tot[u]).astype(BF16))], axis=0)
        rhs = jnp.concatenate([dup(bt.astype(BF16)), dup(kt.astype(BF16))], axis=0)
        sc[u] = _dot_nt(jnp.concatenate([a_u, gr[u]], axis=0), rhs)

    lp, tinv, m_r, mak_v = {}, {}, {}, {}
    for u in units:
        l_ab = jnp.where(strict, sc[u][0:c, 0:2 * c], 0.0)
        m_ak = jnp.where(strict, sc[u][0:c, 2 * c:4 * c], 0.0).astype(BF16)
        m_r[u] = jnp.concatenate([jnp.where(incl, sc[u][c:2 * c, 0:2 * c], 0.0),
                                  jnp.where(incl, sc[u][c:2 * c, 2 * c:4 * c], 0.0)],
                                 axis=1).astype(BF16)
        lp[u] = l_ab.astype(BF16)
        tinv[u] = eye + l_ab
        mak_v[u] = _dot(m_ak, v_s[u]).astype(BF16)
    del sc

    nlev = int(math.log2(c))
    for u in units:
        lp[u] = _dot(lp[u], dup(lp[u])).astype(BF16)
    for lev in range(1, nlev - 1):
        for u in units:
            res = _dot(lp[u], jnp.concatenate([dup(lp[u]), dup(tinv[u].astype(BF16))], axis=1))
            tinv[u] = tinv[u] + res[:, 2 * c:4 * c]
            lp[u] = res[:, 0:2 * c].astype(BF16)
    for u in units:
        tinv[u] = tinv[u] + _dot(lp[u], dup(tinv[u].astype(BF16)))

    u_loc = {}
    for u in units:
        gu = _dot(tinv[u].astype(BF16), jnp.concatenate([a_s[u], dup(mak_v[u])], axis=1))
        gr[u] = jnp.concatenate([gu[:, 0:LANES].astype(BF16), gr[u]], axis=0)
        u_loc[u] = gu[:, LANES:2 * LANES]

    y_u = {}
    for ch in range(nch):
        xs = {}
        for pr in range(npair):
            u = (ch, pr)
            state = s_ref[pr]
            xs[u] = _dot_nt(gr[u], state.astype(BF16))
        for pr in range(npair):
            u = (ch, pr)
            us = (xs[u][0:c] + u_loc[u]).astype(BF16)
            uv = jnp.concatenate([dup(us), v_s[u]], axis=0)
            y_u[u] = xs[u][c:2 * c] + _dot(m_r[u], uv)
            s_ref[pr] = s_ref[pr] * d_tot[u] + _dot_tn(uv, bk[u])

    inv_n = 1.0 / RWKV_N
    for pr in range(npair):
        ls = slice(pr * LANES, (pr + 1) * LANES)
        y = jnp.concatenate([y_u[(ch, pr)] for ch in range(nch)], axis=0)
        mean = seg_sum(y) * inv_n
        yc = y - mean
        var = seg_sum(yc * yc) * inv_n
        yn = yc * lax.rsqrt(var + GN_EPS) * par_ref[8:9, ls] + par_ref[9:10, ls]
        o_ref[:, ls] = ((yn + bonus_p[pr]) * gate_all[:, ls]).astype(o_ref.dtype)


def rwkv_mixer(p, p_lora, par, mu_lora, w2, a2, g2, batch, tp, pad, col0, width):
    nb = tp // ROW_BLOCK
    lw = mu_lora.shape[1]
    cw = RWKV_COLS
    assert col0 % cw == 0 and width % cw == 0 and pad < ROW_BLOCK and p_lora.shape[1] == lw
    nblk = width // cw
    c0 = col0 // cw
    seg = (np.arange(LANES)[:, None] // RWKV_N == np.arange(LANES)[None, :] // RWKV_N)
    seg = jnp.asarray(np.concatenate([seg, seg], axis=0), BF16)
    const = lambda b, t: (0, 0)

    def col_block(cidx):
        return pl.BlockSpec((ROW_BLOCK, cw), lambda b, t: (b * nb + t, cidx))

    return pl.pallas_call(
        functools.partial(_rwkv_kernel, pad=pad, nblk=nblk),
        grid=(batch, nb),
        in_specs=[col_block(c0 + i) for i in range(3 * nblk)] + [
            pl.BlockSpec((ROW_BLOCK, lw), lambda b, t: (b * nb + t, 0)),
            pl.BlockSpec((16, width), const),
            pl.BlockSpec((1, lw), const),
            pl.BlockSpec((DECAY_RANK, width), const),
            pl.BlockSpec((ICLR_RANK, width), const),
            pl.BlockSpec((g2.shape[0], width), const),
            pl.BlockSpec((2 * LANES, LANES), const)],
        out_specs=pl.BlockSpec((ROW_BLOCK, width), lambda b, t: (b * nb + t, 0)),
        out_shape=jax.ShapeDtypeStruct((batch * tp, width), BF16),
        scratch_shapes=[pltpu.VMEM((width // LANES, LANES, LANES), F32),
                        pltpu.VMEM((8, width), F32), pltpu.VMEM((8, width), F32),
                        pltpu.VMEM((8, width), F32), pltpu.VMEM((8, lw), F32)],
        compiler_params=_cparams(("parallel", "arbitrary")),
    )(*([p] * (3 * nblk)), p_lora, par, mu_lora, w2, a2, g2, seg)


def _largest_tile(m, candidates):
    for t in candidates:
        if m % t == 0:
            return t
    return ROW_BLOCK


class _Tiles(NamedTuple):
    in_proj: tuple
    lora: tuple
    mix_out: tuple
    swiglu: tuple
    ffn_out: tuple
    conv_rows: int


def _choose_tiles(m, tp):
    return _Tiles(
        in_proj=(_largest_tile(m, (1664, 1280)), 1024),
        lora=(_largest_tile(m, (1280,)), 512),
        mix_out=(_largest_tile(m, (1664, 1280)), 512),
        swiglu=(_largest_tile(m, (3328, 2080, 1664, 1280)), 256),
        ffn_out=(_largest_tile(m, (640,)), 512),
        conv_rows=_largest_tile(tp, (640,)))


def kernel(x, meta_tokens, norm_mix, w_in, conv_w, attn_sinks, rwkv_mu, rwkv_w0, rwkv_w2, rwkv_a0, rwkv_a2, rwkv_g2, rwkv_k_k, rwkv_k_a, rwkv_r_k, rwkv_ln_w, rwkv_ln_b, w_out, norm_ffn, w_ffn_in, w_ffn_out, norm_final):
    batch, seq, d = x.shape
    depth = w_in.shape[0]
    conv_width = conv_w.shape[1]
    attn_width = ATTN_HEADS * HEAD_DIM
    kv_width = ATTN_KV_HEADS * HEAD_DIM
    rwkv_width = rwkv_w0.shape[1]
    ffn_hidden = w_ffn_out.shape[1]
    assert seq % ROW_BLOCK == 0
    pad = ROW_BLOCK - N_META
    tp = seq + ROW_BLOCK
    nb = tp // ROW_BLOCK
    m = batch * tp

    lora_w = LORA_BLOCK
    gate_pad = -(-GATE_RANK // LANES) * LANES
    col_conv = 0
    col_q = CONV_K * conv_width
    col_k = col_q + attn_width
    col_v = col_k + kv_width
    col_rwkv = col_v + kv_width
    col_lora = col_rwkv + 3 * rwkv_width

    tiles = _choose_tiles(m, tp)

    w_main = (jnp.swapaxes(w_in, 1, 2) * norm_mix[:, None, :]).astype(BF16)
    w_lora = jnp.pad(w_main[:, col_lora:, :],
                     ((0, 0), (0, lora_w - (w_in.shape[2] - col_lora)), (0, 0)))
    w_out_b = w_out.astype(BF16)
    gain_ffn = norm_ffn[:, :, None]
    w_ffn_out_b = w_ffn_out.astype(BF16)

    h, hb, ssq = embed(x.reshape(batch * seq, d), meta_tokens.astype(x.dtype), batch, nb)
    for l in range(depth):
        mu = rwkv_mu[l]
        par = jnp.zeros((16, rwkv_width), F32)
        par = par.at[0].set(mu[0:rwkv_width])
        par = par.at[1].set(mu[rwkv_width:2 * rwkv_width])
        par = par.at[2].set(mu[2 * rwkv_width:3 * rwkv_width])
        par = par.at[3].set(rwkv_w0[l]).at[4].set(rwkv_a0[l]).at[5].set(rwkv_k_k[l])
        par = par.at[6].set(rwkv_k_a[l]).at[7].set(rwkv_r_k[l])
        par = par.at[8].set(rwkv_ln_w[l]).at[9].set(rwkv_ln_b[l])
        mu_lora = jnp.pad(mu[3 * rwkv_width:].reshape(1, -1),
                          ((0, 0), (0, lora_w - (mu.shape[0] - 3 * rwkv_width))))
        g2 = jnp.pad(rwkv_g2[l], ((0, gate_pad - rwkv_g2.shape[1]), (0, 0))).astype(BF16)

        p = normed_matmul(hb, ssq, w_main, l, col_lora, BF16, *tiles.in_proj,
                          vmem_limit=VMEM_LIMIT_HIGH)
        p_lora = normed_matmul(hb, ssq, w_lora, l, lora_w, F32, *tiles.lora)
        y_conv = conv_mixer(p, conv_w[l], batch, tp, pad, col_conv, conv_width, tiles.conv_rows)
        y_attn = swa_mixer(p, attn_sinks[l], batch, tp, pad, col_q, col_k, col_v)
        y_rwkv = rwkv_mixer(p, p_lora, par, mu_lora, rwkv_w2[l].astype(BF16),
                            rwkv_a2[l].astype(BF16), g2, batch, tp, pad, col_rwkv, rwkv_width)
        h, hb, ssq = _residual_call(_mix_out_kernel, [y_conv, y_attn, y_rwkv], w_out_b, l, h,
                                    *tiles.mix_out, vmem_limit=VMEM_LIMIT_HIGH)
        act = normed_swiglu(hb, ssq, gain_ffn, w_ffn_in, l, ffn_hidden, *tiles.swiglu)
        h, hb, ssq = _residual_call(_mm_res_kernel, [act], w_ffn_out_b, l, h, *tiles.ffn_out,
                                    vmem_limit=VMEM_LIMIT_HIGH)

    out = final_norm(h, norm_final, batch, nb, x.dtype)
    return out.reshape(batch, seq, d)
```
